```python
import math
import jax
import jax.numpy as jnp
from jax import lax
import numpy as np

D_MODEL = 2048
BATCH = 2
SEQ = 4096
DEPTH = 2
DEC_BATCH = 8
DEC_SEQ = 8
PAST_LEN = 16384
PAGE_SIZE = 128

N_MIXERS = 2
NORM_EPS = 1e-6

DA_HEAD_DIM = 128
DA_V_DIM = 2 * DA_HEAD_DIM
DA_HEADS = D_MODEL // DA_V_DIM
DA_QK_WIDTH = DA_HEADS * 2 * DA_HEAD_DIM
DA_WIDTH = DA_HEADS * DA_V_DIM
DA_LAYER_INDEX = 1
LAMBDA_INIT = 0.8 - 0.6 * math.exp(-0.3 * (DA_LAYER_INDEX - 1))
Q_BLOCK = 128

RET_HEADS = 8
RET_K_DIM = D_MODEL // RET_HEADS
RET_V_DIM = 2 * RET_K_DIM
RET_QK_WIDTH = RET_HEADS * RET_K_DIM
RET_WIDTH = RET_HEADS * RET_V_DIM
RET_CHUNK = 128
ROPE_BASE = 10000.0

kernel_name = 'hybrid_diffattn_retention_step'


def rms_norm(x, w):
    xf = x.astype(jnp.float32)
    y = xf * lax.rsqrt(jnp.mean(xf * xf, axis=-1, keepdims=True) + NORM_EPS)
    return (y * w.astype(jnp.float32)).astype(x.dtype)


def head_rms_norm(x):
    return x * lax.rsqrt(jnp.mean(x * x, axis=-1, keepdims=True) + NORM_EPS)


def alibi_slopes():
    return jnp.exp2(-8.0 * jnp.arange(1, DA_HEADS + 1, dtype=jnp.float32) / DA_HEADS)


def retention_log_decay():
    return jnp.log1p(-jnp.exp2(-5.0 - jnp.arange(RET_HEADS, dtype=jnp.float32)))


def rotate(x, pos):
    half = x.shape[-1] // 2
    inv_freq = 1.0 / (ROPE_BASE ** jnp.linspace(0.0, 1.0, half, dtype=jnp.float32))
    ang = pos.astype(jnp.float32)[:, None] * inv_freq[None, :]
    cos = jnp.cos(ang)[None, :, None, :]
    sin = jnp.sin(ang)[None, :, None, :]
    xf = x.astype(jnp.float32)
    x1, x2 = xf[..., :half], xf[..., half:]
    return jnp.concatenate([x1 * cos - x2 * sin, x1 * sin + x2 * cos], axis=-1)


def diff_lambda(lq1, lk1, lq2, lk2):
    f = jnp.float32
    return (jnp.exp(jnp.sum(lq1.astype(f) * lk1.astype(f)))
            - jnp.exp(jnp.sum(lq2.astype(f) * lk2.astype(f))) + LAMBDA_INIT)


def diff_attn_project(h, w_in):
    b, s, _ = h.shape
    proj = h @ w_in
    q = proj[..., :DA_QK_WIDTH].reshape(b, s, DA_HEADS, 2, DA_HEAD_DIM)
    k = proj[..., DA_QK_WIDTH:2 * DA_QK_WIDTH].reshape(b, s, DA_HEADS, 2, DA_HEAD_DIM)
    v = proj[..., 2 * DA_QK_WIDTH:2 * DA_QK_WIDTH + DA_WIDTH].reshape(b, s, DA_HEADS, DA_V_DIM)
    g = proj[..., 2 * DA_QK_WIDTH + DA_WIDTH:]
    return q, k, v, g


def diff_attn_prompt(q, k, v, lam, slopes):
    b, s, nh, _, d = q.shape
    nb = s // Q_BLOCK
    scale = d ** -0.5
    qb = q.reshape(b, nb, Q_BLOCK, nh, 2, d).transpose(1, 0, 2, 3, 4, 5)
    pos_k = jnp.arange(s, dtype=jnp.int32)

    def block(args):
        q_blk, blk = args
        pos_q = blk * Q_BLOCK + jnp.arange(Q_BLOCK, dtype=jnp.int32)
        dist = (pos_q[:, None] - pos_k[None, :]).astype(jnp.float32)
        sc = jnp.einsum('bqhcd,bkhcd->bchqk', q_blk, k).astype(jnp.float32) * scale
        sc = sc - slopes[:, None, None] * dist[None]
        sc = jnp.where(dist >= 0, sc, -jnp.inf)
        p = jax.nn.softmax(sc, axis=-1)
        p = p[:, 0] - lam * p[:, 1]
        return jnp.einsum('bhqk,bkhe->bqhe', p.astype(v.dtype), v)

    o = lax.map(block, (qb, jnp.arange(nb, dtype=jnp.int32)))
    return o.transpose(1, 0, 2, 3, 4).reshape(b, s, nh, -1)


def diff_attn_sample(q, k_new, v_new, cache_k, cache_v, page_table, lam, slopes):
    db, t, nh, _, d = q.shape
    past = page_table.shape[1] * cache_k.shape[1]
    scale = d ** -0.5
    kp = cache_k[page_table].reshape(db, past, nh, 2, d)
    vp = cache_v[page_table].reshape(db, past, nh, -1)
    pos_q = past + jnp.arange(t, dtype=jnp.int32)
    pos_k = jnp.arange(past + t, dtype=jnp.int32)
    dist = (pos_q[:, None] - pos_k[None, :]).astype(jnp.float32)
    s_past = jnp.einsum('bqhcd,bkhcd->bchqk', q, kp)
    s_new = jnp.einsum('bqhcd,bkhcd->bchqk', q, k_new)
    sc = jnp.concatenate([s_past, s_new], axis=-1).astype(jnp.float32) * scale
    sc = sc - slopes[:, None, None] * dist[None]
    sc = jnp.where(dist >= 0, sc, -jnp.inf)
    p = jax.nn.softmax(sc, axis=-1)
    p = (p[:, 0] - lam * p[:, 1]).astype(v_new.dtype)
    return (jnp.einsum('bhqk,bkhe->bqhe', p[..., :past], vp)
            + jnp.einsum('bhqk,bkhe->bqhe', p[..., past:], v_new))


def diff_attn_output(o, g, subln_w, w_out):
    b, s = o.shape[:2]
    of = head_rms_norm(o.astype(jnp.float32)) * subln_w.astype(jnp.float32) * (1.0 - LAMBDA_INIT)
    gated = of.reshape(b, s, DA_WIDTH).astype(g.dtype) * jax.nn.silu(g)
    return gated @ w_out


def ret_project(h, w_in, pos):
    b, s, _ = h.shape
    proj = h @ w_in
    q = rotate(proj[..., :RET_QK_WIDTH].reshape(b, s, RET_HEADS, RET_K_DIM), pos)
    k = rotate(proj[..., RET_QK_WIDTH:2 * RET_QK_WIDTH].reshape(b, s, RET_HEADS, RET_K_DIM), pos) * (RET_K_DIM ** -0.5)
    v = proj[..., 2 * RET_QK_WIDTH:2 * RET_QK_WIDTH + RET_WIDTH].reshape(b, s, RET_HEADS, RET_V_DIM).astype(jnp.float32)
    g = proj[..., 2 * RET_QK_WIDTH + RET_WIDTH:]
    return q.transpose(0, 2, 1, 3), k.transpose(0, 2, 1, 3), v.transpose(0, 2, 1, 3), g


def retention_chunk(s_prev, q, k, v, log_g):
    c = q.shape[2]
    idx = jnp.arange(c, dtype=jnp.float32)
    diff = idx[:, None] - idx[None, :]
    decay = jnp.where(diff >= 0, jnp.exp(log_g[:, None, None] * jnp.maximum(diff, 0.0)), 0.0)
    scores = jnp.einsum('bhid,bhjd->bhij', q, k) * decay
    o = (jnp.einsum('bhij,bhje->bhie', scores, v)
         + jnp.einsum('bhid,bhde->bhie', q, s_prev) * jnp.exp(log_g[:, None] * (idx + 1.0))[:, :, None])
    k_dec = k * jnp.exp(log_g[:, None] * (c - 1.0 - idx))[:, :, None]
    s_new = jnp.exp(log_g * c)[:, None, None] * s_prev + jnp.einsum('bhjd,bhje->bhde', k_dec, v)
    return s_new, o


def retention_prompt(q, k, v, log_g):
    b, nh, s, dk = q.shape
    dv = v.shape[-1]
    nc = s // RET_CHUNK

    def chunks(t):
        return t.reshape(b, nh, nc, RET_CHUNK, t.shape[-1]).transpose(2, 0, 1, 3, 4)

    def step(state, xs):
        qc, kc, vc = xs
        return retention_chunk(state, qc, kc, vc, log_g)

    s0 = jnp.zeros((b, nh, dk, dv), jnp.float32)
    s_fin, oc = lax.scan(step, s0, (chunks(q), chunks(k), chunks(v)))
    o = oc.transpose(1, 2, 0, 3, 4).reshape(b, nh, s, dv)
    return o, s_fin


def ret_output(o, g, w_out):
    b, _, s, _ = o.shape
    on = head_rms_norm(o).transpose(0, 2, 1, 3).reshape(b, s, RET_WIDTH)
    return (on.astype(g.dtype) * jax.nn.silu(g)) @ w_out


def setup_inputs(seed: int = 0) -> dict:
    key = jax.random.key(seed)
    ks = jax.random.split(key, 20)
    f = jnp.float32
    n_pages = PAST_LEN // PAGE_SIZE
    n_used = DEC_BATCH * n_pages
    n_phys = n_used + (n_used + 3) // 4
    in_attn = 2 * DA_QK_WIDTH + 2 * DA_WIDTH
    in_ret = 2 * RET_QK_WIDTH + 2 * RET_WIDTH

    def gain(k, n):
        return 1.0 + 0.05 * jax.random.normal(k, (n,), f)

    page_table = jax.random.permutation(ks[5], n_phys)[:n_used].reshape(DEC_BATCH, n_pages).astype(jnp.int32)
    return {
        'x_prompt': jax.random.normal(ks[0], (BATCH, SEQ, D_MODEL), f),
        'x_sample': jax.random.normal(ks[1], (DEC_BATCH, DEC_SEQ, D_MODEL), f),
        'cache_k': jax.random.normal(ks[2], (n_phys, PAGE_SIZE, DA_HEADS, 2 * DA_HEAD_DIM), f),
        'cache_v': jax.random.normal(ks[3], (n_phys, PAGE_SIZE, DA_HEADS, DA_V_DIM), f),
        'page_table': page_table,
        'state_ret': 0.5 * jax.random.normal(ks[4], (DEC_BATCH, RET_HEADS, RET_K_DIM, RET_V_DIM), f),
        'norm_pre_attn': gain(ks[6], D_MODEL),
        'w_in_attn': jax.random.normal(ks[7], (D_MODEL, in_attn), f) * D_MODEL ** -0.5,
        'lambda_q1': 0.1 * jax.random.normal(ks[8], (DA_HEAD_DIM,), f),
        'lambda_k1': 0.1 * jax.random.normal(ks[9], (DA_HEAD_DIM,), f),
        'lambda_q2': 0.1 * jax.random.normal(ks[10], (DA_HEAD_DIM,), f),
        'lambda_k2': 0.1 * jax.random.normal(ks[11], (DA_HEAD_DIM,), f),
        'subln_w_attn': gain(ks[12], DA_V_DIM),
        'w_out_attn': jax.random.normal(ks[13], (DA_WIDTH, D_MODEL), f) * DA_WIDTH ** -0.5,
        'norm_post_attn': gain(ks[14], D_MODEL),
        'norm_pre_ret': gain(ks[15], D_MODEL),
        'w_in_ret': jax.random.normal(ks[16], (D_MODEL, in_ret), f) * D_MODEL ** -0.5,
        'w_out_ret': jax.random.normal(ks[17], (RET_WIDTH, D_MODEL), f) * RET_WIDTH ** -0.5,
        'norm_post_ret': gain(ks[18], D_MODEL),
    }


def reference(x_prompt, x_sample, cache_k, cache_v, page_table, state_ret,
              norm_pre_attn, w_in_attn, lambda_q1, lambda_k1, lambda_q2, lambda_k2,
              subln_w_attn, w_out_attn, norm_post_attn,
              norm_pre_ret, w_in_ret, w_out_ret, norm_post_ret):
    slopes = alibi_slopes()
    lam = diff_lambda(lambda_q1, lambda_k1, lambda_q2, lambda_k2)
    log_g = retention_log_decay()
    pos_prompt = jnp.arange(SEQ, dtype=jnp.int32)
    pos_sample = PAST_LEN + jnp.arange(DEC_SEQ, dtype=jnp.int32)
    y_prompt, y_sample = x_prompt, x_sample
    for layer in range(DEPTH):
        if layer % N_MIXERS == 0:
            hp = rms_norm(y_prompt, norm_pre_attn)
            hs = rms_norm(y_sample, norm_pre_attn)
            qp, kp, vp, gp = diff_attn_project(hp, w_in_attn)
            qs, ks_, vs, gs = diff_attn_project(hs, w_in_attn)
            op = diff_attn_prompt(qp, kp, vp, lam, slopes)
            osm = diff_attn_sample(qs, ks_, vs, cache_k, cache_v, page_table, lam, slopes)
            y_prompt = y_prompt + rms_norm(diff_attn_output(op, gp, subln_w_attn, w_out_attn), norm_post_attn)
            y_sample = y_sample + rms_norm(diff_attn_output(osm, gs, subln_w_attn, w_out_attn), norm_post_attn)
            k_prompt = kp.reshape(kp.shape[0], kp.shape[1], DA_HEADS, 2 * DA_HEAD_DIM)
            v_prompt = vp
            k_sample = ks_.reshape(ks_.shape[0], ks_.shape[1], DA_HEADS, 2 * DA_HEAD_DIM)
            v_sample = vs
        else:
            hp = rms_norm(y_prompt, norm_pre_ret)
            hs = rms_norm(y_sample, norm_pre_ret)
            qp, kp, vp, gp = ret_project(hp, w_in_ret, pos_prompt)
            qs, ks_, vs, gs = ret_project(hs, w_in_ret, pos_sample)
            op, sp = retention_prompt(qp, kp, vp, log_g)
            ss, osm = retention_chunk(state_ret.astype(jnp.float32), qs, ks_, vs, log_g)
            y_prompt = y_prompt + rms_norm(ret_output(op, gp, w_out_ret), norm_post_ret)
            y_sample = y_sample + rms_norm(ret_output(osm, gs, w_out_ret), norm_post_ret)
            ret_state_prompt = sp.astype(state_ret.dtype)
            ret_state_sample = ss.astype(state_ret.dtype)
    return (y_prompt, y_sample, k_prompt, v_prompt, ret_state_prompt, k_sample, v_sample, ret_state_sample)
```

```python
import functools
import math

import jax
import jax.numpy as jnp
from jax import lax
from jax.experimental import pallas as pl
from jax.experimental.pallas import tpu as pltpu

F32 = jnp.float32
BF16 = jnp.bfloat16

NORM_EPS = 1e-6
LAMBDA_INIT = 0.8 - 0.6 * math.exp(-0.3 * 0)
ROPE_BASE = 10000.0
HEAD_W = 256
HALF = 128
BF16_ROWS = 16
VMEM_LIMIT_BYTES = 56 * 1024 * 1024

_SMEM = pl.BlockSpec(memory_space=pltpu.SMEM)


def _cparams(sem):
    return pltpu.CompilerParams(dimension_semantics=sem, vmem_limit_bytes=VMEM_LIMIT_BYTES)


def _silu(g):
    return g * (1.0 / (1.0 + jnp.exp(-g)))


def _dot_nt(a, b):
    return lax.dot_general(a, b, (((1,), (1,)), ((), ())), preferred_element_type=F32)


def _dot_tn(a, b):
    return lax.dot_general(a, b, (((0,), (0,)), ((), ())), preferred_element_type=F32)


def _dot(a, b):
    return jnp.dot(a, b, preferred_element_type=F32)


def _pad_rows(x, n):
    if x.shape[0] == n:
        return x
    return jnp.concatenate([x, jnp.zeros((n - x.shape[0], x.shape[1]), x.dtype)], axis=0)


def _seg_index_map(j0, ntiles):
    def index_map(i, j):
        return (i, jnp.minimum(jnp.maximum(j - j0, 0), ntiles - 1))
    return index_map


def _inproj_kernel(*refs, seg_tiles, seg_kinds, tn, rotate_scale):
    n_in = 5 if rotate_scale is not None else 3
    x_ref, nw_ref, w_ref = refs[:3]
    cos_ref = sin_ref = None
    if rotate_scale is not None:
        cos_ref, sin_ref = refs[3:5]
    h_ref = refs[-1]
    out_refs = refs[n_in:-1]
    j = pl.program_id(1)

    @pl.when(j == 0)
    def _():
        x = x_ref[...]
        ms = jnp.mean(x * x, axis=-1, keepdims=True)
        h_ref[...] = (x * lax.rsqrt(ms + NORM_EPS) * nw_ref[...]).astype(BF16)

    acc = _dot(h_ref[...], w_ref[...])

    def rotated(a, scale):
        cos = cos_ref[...]
        sin = sin_ref[...]
        parts = []
        for hh in range(tn // HEAD_W):
            x1 = a[:, hh * HEAD_W: hh * HEAD_W + HALF]
            x2 = a[:, hh * HEAD_W + HALF: (hh + 1) * HEAD_W]
            parts.append((x1 * cos - x2 * sin) * scale)
            parts.append((x1 * sin + x2 * cos) * scale)
        return jnp.concatenate(parts, axis=-1)

    j0 = 0
    oi = 0
    for ntiles, kinds in zip(seg_tiles, seg_kinds):
        outs = []
        for kind in kinds:
            outs.append((out_refs[oi], kind))
            oi += 1

        @pl.when((j >= j0) & (j < j0 + ntiles))
        def _(outs=outs):
            for ref, kind in outs:
                if kind == "rot_q":
                    ref[...] = rotated(acc, rotate_scale[0]).astype(ref.dtype)
                elif kind == "rot_k":
                    ref[...] = rotated(acc, rotate_scale[1]).astype(ref.dtype)
                else:
                    ref[...] = acc.astype(ref.dtype)

        j0 += ntiles


def _norm_inproj(x, nw, w, segs, *, tm, tn, cos=None, sin=None, rotate_scale=None, name):
    m, d = x.shape
    n = w.shape[1]
    tm = min(tm, m)
    seg_tiles = tuple(width // tn for width, _ in segs)
    seg_kinds = tuple(tuple(kind for _, kind in outs) for _, outs in segs)
    out_shapes = []
    out_specs = []
    j0 = 0
    for (width, outs), ntiles in zip(segs, seg_tiles):
        for dtype, _ in outs:
            out_shapes.append(jax.ShapeDtypeStruct((m, width), dtype))
            out_specs.append(pl.BlockSpec((tm, tn), _seg_index_map(j0, ntiles)))
        j0 += ntiles
    assert j0 * tn == n
    in_specs = [
        pl.BlockSpec((tm, d), lambda i, j: (i, 0)),
        pl.BlockSpec((1, d), lambda i, j: (0, 0)),
        pl.BlockSpec((d, tn), lambda i, j: (0, j)),
    ]
    args = [x, nw.reshape(1, d), w]
    if rotate_scale is not None:
        nblk = cos.shape[0] // tm
        in_specs += [pl.BlockSpec((tm, HALF), lambda i, j: (i % nblk, 0))] * 2
        args += [cos, sin]
    kern = functools.partial(_inproj_kernel, seg_tiles=seg_tiles, seg_kinds=seg_kinds, tn=tn,
                             rotate_scale=rotate_scale)
    return pl.pallas_call(
        kern,
        out_shape=out_shapes,
        grid=(m // tm, n // tn),
        in_specs=in_specs,
        out_specs=out_specs,
        scratch_shapes=[pltpu.VMEM((tm, d), BF16)],
        compiler_params=_cparams(("parallel", "arbitrary")),
        name=name,
    )(*args)


def _outproj_kernel(a_ref, w_ref, x_ref, nw_ref, o_ref, acc_ref):
    kk = pl.program_id(1)

    @pl.when(kk == 0)
    def _():
        acc_ref[...] = jnp.zeros_like(acc_ref)

    acc_ref[...] += _dot(a_ref[...].astype(BF16), w_ref[...])

    @pl.when(kk == pl.num_programs(1) - 1)
    def _():
        z = acc_ref[...]
        ms = jnp.mean(z * z, axis=-1, keepdims=True)
        o_ref[...] = x_ref[...] + z * lax.rsqrt(ms + NORM_EPS) * nw_ref[...]


def _outproj_norm_residual(a, w, x, nw, *, tm, tk, name):
    m, kd = a.shape
    d = w.shape[1]
    tm = min(tm, m)
    return pl.pallas_call(
        _outproj_kernel,
        out_shape=jax.ShapeDtypeStruct((m, d), F32),
        grid=(m // tm, kd // tk),
        in_specs=[
            pl.BlockSpec((tm, tk), lambda i, k: (i, k)),
            pl.BlockSpec((tk, d), lambda i, k: (k, 0)),
            pl.BlockSpec((tm, d), lambda i, k: (i, 0)),
            pl.BlockSpec((1, d), lambda i, k: (0, 0)),
        ],
        out_specs=pl.BlockSpec((tm, d), lambda i, k: (i, 0)),
        scratch_shapes=[pltpu.VMEM((tm, d), F32)],
        compiler_params=_cparams(("parallel", "arbitrary")),
        name=name,
    )(a, w, x, nw.reshape(1, d))


def _diff_lambda(lq1_ref, lk1_ref, lq2_ref, lk2_ref):
    a = jnp.sum(lq1_ref[...] * lk1_ref[...], axis=-1, keepdims=True)
    b = jnp.sum(lq2_ref[...] * lk2_ref[...], axis=-1, keepdims=True)
    return jnp.exp(a) - jnp.exp(b) + LAMBDA_INIT


def _subln_gate(o, g, subln):
    ms = jnp.mean(o * o, axis=-1, keepdims=True)
    of = o * lax.rsqrt(ms + NORM_EPS) * subln * (1.0 - LAMBDA_INIT)
    return of * _silu(g)


def _attn_prompt_kernel(slopes_ref, q_ref, k_ref, v_ref, g_ref, subln_ref,
                        lq1_ref, lk1_ref, lq2_ref, lk2_ref, o_ref, *, tq, tk):
    h = pl.program_id(1)
    qi = pl.program_id(2)
    slope = slopes_ref[h]
    scale = HALF ** -0.5
    q = q_ref[0]
    qc = (q[:, :HALF], q[:, HALF:])
    q0 = qi * tq

    def kv_step(j, carry, masked):
        k0 = pl.multiple_of(j * tk, tk)
        kblk = k_ref[0, pl.ds(k0, tk), :]
        vblk = v_ref[0, pl.ds(k0, tk), :]
        col = lax.broadcasted_iota(jnp.int32, (1, tk), 1) + (k0 - q0)
        bias = slope * col.astype(F32)
        if masked:
            row = lax.broadcasted_iota(jnp.int32, (tq, 1), 0)
            valid = row >= col
        new = []
        for c in range(2):
            m, l, acc = carry[c]
            s = _dot_nt(qc[c], kblk[:, c * HALF:(c + 1) * HALF]) * scale + bias
            if masked:
                s = jnp.where(valid, s, -jnp.inf)
            m_new = jnp.maximum(m, jnp.max(s, axis=-1, keepdims=True))
            alpha = jnp.exp(m - m_new)
            p = jnp.exp(s - m_new)
            l = alpha * l + jnp.sum(p, axis=-1, keepdims=True)
            acc = alpha * acc + _dot(p.astype(BF16), vblk)
            new.append((m_new, l, acc))
        return tuple(new)

    init = tuple((jnp.full((tq, 1), -jnp.inf, F32), jnp.zeros((tq, 1), F32),
                  jnp.zeros((tq, HEAD_W), F32)) for _ in range(2))
    n_full = qi * (tq // tk)
    carry = lax.fori_loop(0, n_full, functools.partial(kv_step, masked=False), init)
    for jj in range(tq // tk):
        carry = kv_step(n_full + jj, carry, masked=True)

    lam = _diff_lambda(lq1_ref, lk1_ref, lq2_ref, lk2_ref)
    (_, l1, a1), (_, l2, a2) = carry
    o = a1 / l1 - lam * (a2 / l2)
    o_ref[0] = _subln_gate(o, g_ref[0], subln_ref[...]).astype(o_ref.dtype)


def _attn_prompt(slopes, q, kb, vb, g, subln, lams, *, tq, tk):
    b, s, w = q.shape
    nh = w // HEAD_W
    blk_q = pl.BlockSpec((1, tq, HEAD_W), lambda bi, hi, qi: (bi, qi, hi))
    blk_kv = pl.BlockSpec((1, s, HEAD_W), lambda bi, hi, qi: (bi, 0, hi))
    vec = lambda n: pl.BlockSpec((1, n), lambda bi, hi, qi: (0, 0))
    kern = functools.partial(_attn_prompt_kernel, tq=tq, tk=tk)
    return pl.pallas_call(
        kern,
        out_shape=jax.ShapeDtypeStruct((b, s, w), BF16),
        grid=(b, nh, s // tq),
        in_specs=[_SMEM, blk_q, blk_kv, blk_kv, blk_q, vec(HEAD_W)] + [vec(HALF)] * 4,
        out_specs=blk_q,
        compiler_params=_cparams(("parallel", "parallel", "arbitrary")),
        name="diff_attn_prompt",
    )(slopes, q, kb, vb, g, subln.reshape(1, HEAD_W), *[v.reshape(1, HALF) for v in lams])


def _attn_sample_kernel(pt_ref, slopes_ref, q_ref, knew_ref, vnew_ref, g_ref, subln_ref,
                        lq1_ref, lk1_ref, lq2_ref, lk2_ref, *rest,
                        pages_per_step, n_steps, nh, t, page, past):
    del pt_ref
    pp = pages_per_step
    k_refs = rest[:pp]
    v_refs = rest[pp:2 * pp]
    o_ref = rest[2 * pp]
    qx_ref, s_ref, p_ref, acc_ref = rest[2 * pp + 1:]
    step = pl.program_id(1)
    scale = HALF ** -0.5
    rows = 2 * t
    tok = lax.broadcasted_iota(jnp.int32, (rows, 1), 0) % t

    def head_rows(ref, h):
        return ref[:, h, :]

    @pl.when(step == 0)
    def _():
        q = q_ref[...]
        z = jnp.zeros((t, HALF), F32)
        for h in range(nh):
            q1 = q[:, h * HEAD_W: h * HEAD_W + HALF]
            q2 = q[:, h * HEAD_W + HALF: (h + 1) * HEAD_W]
            top = jnp.concatenate([q1, z], axis=-1)
            bot = jnp.concatenate([z, q2], axis=-1)
            qx_ref[h] = jnp.concatenate([top, bot], axis=0).astype(BF16)
            acc_ref[h] = jnp.zeros((rows, HEAD_W), F32)

    def scores(h, kh, kpos0):
        st = _dot_nt(qx_ref[h], kh.astype(BF16)) * scale
        kpos = lax.broadcasted_iota(jnp.int32, (1, page), 1) + kpos0
        dist = (past + tok) - kpos
        st = st - slopes_ref[h] * dist.astype(F32)
        return jnp.where(dist >= 0, st, -jnp.inf)

    @pl.when(step < n_steps)
    def _():
        for r in range(pp):
            pg = step * pp + r
            off = pl.multiple_of(pg * page, page)
            for h in range(nh):
                s_ref[h, :, pl.ds(off, page)] = scores(h, head_rows(k_refs[r], h), pg * page)

    @pl.when(step == n_steps - 1)
    def _():
        lam = _diff_lambda(lq1_ref, lk1_ref, lq2_ref, lk2_ref)
        for h in range(nh):
            kh = _pad_rows(head_rows(knew_ref, h), page)
            sn = scores(h, kh, past)
            sn = jnp.where(lax.broadcasted_iota(jnp.int32, (1, page), 1) < t, sn, -jnp.inf)
            s_ref[h, :, pl.ds(past, page)] = sn
            s = s_ref[h]
            m = jnp.max(s, axis=-1, keepdims=True)
            e = jnp.exp(s - m)
            pn = e / jnp.sum(e, axis=-1, keepdims=True)
            p_ref[h] = pn[:t] - lam * pn[t:]

    @pl.when(step >= n_steps)
    def _():
        for r in range(pp):
            pg = (step - n_steps) * pp + r
            off = pl.multiple_of(pg * page, page)
            for h in range(nh):
                ph = _pad_rows(p_ref[h, :, pl.ds(off, page)], rows).astype(BF16)
                vh = head_rows(v_refs[r], h).astype(BF16)
                acc_ref[h] += _dot(ph, vh)

    @pl.when(step == 2 * n_steps - 1)
    def _():
        g = g_ref[...]
        for h in range(nh):
            ph = _pad_rows(p_ref[h, :, pl.ds(past, page)], rows).astype(BF16)
            vh = _pad_rows(head_rows(vnew_ref, h), page).astype(BF16)
            o = (acc_ref[h] + _dot(ph, vh))[:t]
            o_ref[:, h * HEAD_W:(h + 1) * HEAD_W] = _subln_gate(
                o, g[:, h * HEAD_W:(h + 1) * HEAD_W], subln_ref[...])


def _attn_sample(page_table, slopes, q, k_new, v_new, g, cache_k, cache_v, subln, lams,
                 *, pages_per_step):
    db, t, w = q.shape
    nh = w // HEAD_W
    n_phys, page = cache_k.shape[:2]
    n_pages = page_table.shape[1]
    past = n_pages * page
    pp = pages_per_step
    n_steps = n_pages // pp
    assert 2 * t == BF16_ROWS and n_steps * pp == n_pages

    def k_map(r):
        return lambda b, s, pt: (pt[b, jnp.minimum(s, n_steps - 1) * pp + r], 0, 0, 0)

    def v_map(r):
        return lambda b, s, pt: (pt[b, jnp.maximum(s - n_steps, 0) * pp + r], 0, 0, 0)

    per_b3 = lambda shape: pl.BlockSpec((None,) + shape, lambda b, s, pt: (b, 0, 0))
    per_b4 = lambda shape: pl.BlockSpec((None,) + shape, lambda b, s, pt: (b, 0, 0, 0))
    vec = lambda n: pl.BlockSpec((1, n), lambda b, s, pt: (0, 0))
    page_blk = (None, page, nh, HEAD_W)
    in_specs = ([_SMEM, per_b3((t, w)), per_b4((t, nh, HEAD_W)), per_b4((t, nh, HEAD_W)),
                 per_b3((t, w)), vec(HEAD_W)] + [vec(HALF)] * 4
                + [pl.BlockSpec(page_blk, k_map(r)) for r in range(pp)]
                + [pl.BlockSpec(page_blk, v_map(r)) for r in range(pp)])
    kern = functools.partial(_attn_sample_kernel, pages_per_step=pp, n_steps=n_steps, nh=nh, t=t,
                             page=page, past=past)
    nk = past + page
    return pl.pallas_call(
        kern,
        out_shape=jax.ShapeDtypeStruct((db, t, w), F32),
        grid_spec=pltpu.PrefetchScalarGridSpec(
            num_scalar_prefetch=1,
            grid=(db, 2 * n_steps),
            in_specs=in_specs,
            out_specs=per_b3((t, w)),
            scratch_shapes=[
                pltpu.VMEM((nh, 2 * t, HEAD_W), BF16),
                pltpu.VMEM((nh, 2 * t, nk), F32),
                pltpu.VMEM((nh, t, nk), F32),
                pltpu.VMEM((nh, 2 * t, HEAD_W), F32),
            ],
        ),
        compiler_params=_cparams(("parallel", "arbitrary")),
        name="diff_attn_sample",
    )(page_table, slopes, q, k_new, v_new, g, subln.reshape(1, HEAD_W),
      *[v.reshape(1, HALF) for v in lams], *([cache_k] * pp), *([cache_v] * pp))


def _retention_kernel(logg_ref, q_ref, k_ref, v_ref, g_ref, *rest, c_true, has_init):
    if has_init:
        s0_ref, o_ref, s_ref = rest
    else:
        s0_ref = None
        o_ref, s_ref = rest
    h = pl.program_id(1)
    ci = pl.program_id(2)
    lg = jnp.full((1, 1), logg_ref[h], F32)
    c = max(c_true, BF16_ROWS)

    @pl.when(ci == 0)
    def _():
        if has_init:
            s_ref[0, 0] = s0_ref[0, 0]
        else:
            s_ref[0, 0] = jnp.zeros(s_ref.shape[2:], F32)

    q = _pad_rows(q_ref[0], c).astype(BF16)
    kf = _pad_rows(k_ref[0], c).astype(F32)
    v = _pad_rows(v_ref[0], c).astype(BF16)
    state = s_ref[0, 0]

    ri = lax.broadcasted_iota(jnp.int32, (c, 1), 0)
    cj = lax.broadcasted_iota(jnp.int32, (1, c), 1)
    diff = ri - cj
    decay = jnp.where(diff >= 0, jnp.exp(lg * jnp.maximum(diff, 0).astype(F32)), 0.0)
    rif = ri.astype(F32)
    scores = _dot_nt(q, kf.astype(BF16)) * decay
    o = _dot(scores.astype(BF16), v) + _dot(q, state.astype(BF16)) * jnp.exp(lg * (rif + 1.0))
    k_dec = (kf * jnp.exp(lg * (c_true - 1.0 - rif))).astype(BF16)
    s_ref[0, 0] = jnp.exp(lg * float(c_true)) * state + _dot_tn(k_dec, v)

    o = o[:c_true]
    ms = jnp.mean(o * o, axis=-1, keepdims=True)
    o_ref[0] = (o * lax.rsqrt(ms + NORM_EPS) * _silu(g_ref[0])).astype(o_ref.dtype)


def _retention(logg, q, k, v, g, s0, *, chunk, out_dtype):
    b, s, wqk = q.shape
    nh = logg.shape[0]
    dk = wqk // nh
    dv = v.shape[2] // nh
    blk = lambda wd: pl.BlockSpec((1, chunk, wd), lambda bi, hi, ci: (bi, ci, hi))
    st_blk = pl.BlockSpec((1, 1, dk, dv), lambda bi, hi, ci: (bi, hi, 0, 0))
    in_specs = [_SMEM, blk(dk), blk(dk), blk(dv), blk(dv)]
    args = [logg, q, k, v, g]
    if s0 is not None:
        in_specs.append(st_blk)
        args.append(s0)
    kern = functools.partial(_retention_kernel, c_true=chunk, has_init=s0 is not None)
    return pl.pallas_call(
        kern,
        out_shape=[jax.ShapeDtypeStruct((b, s, nh * dv), out_dtype),
                   jax.ShapeDtypeStruct((b, nh, dk, dv), F32)],
        grid=(b, nh, s // chunk),
        in_specs=in_specs,
        out_specs=[blk(dv), st_blk],
        compiler_params=_cparams(("parallel", "parallel", "arbitrary")),
        name="retention",
    )(*args)


def _rope_tables(pos, half):
    inv_freq = 1.0 / (ROPE_BASE ** jnp.linspace(0.0, 1.0, half, dtype=F32))
    ang = pos.astype(F32)[:, None] * inv_freq[None, :]
    return jnp.cos(ang), jnp.sin(ang)


def kernel(x_prompt, x_sample, cache_k, cache_v, page_table, state_ret, norm_pre_attn, w_in_attn, lambda_q1, lambda_k1, lambda_q2, lambda_k2, subln_w_attn, w_out_attn, norm_post_attn, norm_pre_ret, w_in_ret, w_out_ret, norm_post_ret):
    b, s, d = x_prompt.shape
    db, t, _ = x_sample.shape
    nh_a = cache_k.shape[2]
    nh_r = state_ret.shape[1]
    dk_r, dv_r = state_ret.shape[2:]
    past = page_table.shape[1] * cache_k.shape[1]
    wa = nh_a * HEAD_W
    wqk_r, wv_r = nh_r * dk_r, nh_r * dv_r

    slopes = jnp.exp2(-8.0 * jnp.arange(1, nh_a + 1, dtype=F32) / nh_a)
    logg = jnp.log1p(-jnp.exp2(-5.0 - jnp.arange(nh_r, dtype=F32)))
    lams = (lambda_q1, lambda_k1, lambda_q2, lambda_k2)
    w_in_a = w_in_attn.astype(BF16)
    w_out_a = w_out_attn.astype(BF16)
    w_in_r = w_in_ret.astype(BF16)
    w_out_r = w_out_ret.astype(BF16)

    xp = x_prompt.reshape(b * s, d)
    xs = x_sample.reshape(db * t, d)

    segs_p = [(wa, [(BF16, "q")]), (wa, [(F32, "k"), (BF16, "kb")]),
              (wa, [(F32, "v"), (BF16, "vb")]), (wa, [(F32, "g")])]
    qp, kp, kbp, vp, vbp, gp = _norm_inproj(xp, norm_pre_attn, w_in_a, segs_p, tm=1024, tn=512,
                                            name="inproj_attn_prompt")
    segs_s = [(wa, [(F32, "q")]), (wa, [(F32, "k")]), (wa, [(F32, "v")]), (wa, [(F32, "g")])]
    qs, ks, vs, gs = _norm_inproj(xs, norm_pre_attn, w_in_a, segs_s, tm=1024, tn=512,
                                  name="inproj_attn_sample")

    r3 = lambda a, n: a.reshape(n, -1, a.shape[-1])
    gated_p = _attn_prompt(slopes, r3(qp, b), r3(kbp, b), r3(vbp, b), r3(gp, b), subln_w_attn, lams,
                           tq=512, tk=256)
    gated_s = _attn_sample(page_table, slopes, r3(qs, db), ks.reshape(db, t, nh_a, HEAD_W),
                           vs.reshape(db, t, nh_a, HEAD_W), r3(gs, db), cache_k, cache_v,
                           subln_w_attn, lams, pages_per_step=4)

    y1p = _outproj_norm_residual(gated_p.reshape(b * s, wa), w_out_a, xp, norm_post_attn,
                                 tm=512, tk=2048, name="outproj_attn_prompt")
    y1s = _outproj_norm_residual(gated_s.reshape(db * t, wa), w_out_a, xs, norm_post_attn,
                                 tm=512, tk=2048, name="outproj_attn_sample")

    cos_p, sin_p = _rope_tables(jnp.arange(s, dtype=jnp.int32), dk_r // 2)
    cos_s, sin_s = _rope_tables(past + jnp.arange(t, dtype=jnp.int32), dk_r // 2)
    cos_s, sin_s = jnp.tile(cos_s, (db, 1)), jnp.tile(sin_s, (db, 1))
    rs = (1.0, dk_r ** -0.5)
    segs_rp = [(wqk_r, [(BF16, "rot_q")]), (wqk_r, [(BF16, "rot_k")]),
               (wv_r, [(BF16, "v")]), (wv_r, [(F32, "g")])]
    rq, rk, rv, rg = _norm_inproj(y1p, norm_pre_ret, w_in_r, segs_rp, tm=1024, tn=512,
                                  cos=cos_p, sin=sin_p, rotate_scale=rs, name="inproj_ret_prompt")
    segs_rs = [(wqk_r, [(F32, "rot_q")]), (wqk_r, [(F32, "rot_k")]),
               (wv_r, [(F32, "v")]), (wv_r, [(F32, "g")])]
    sq, sk, sv, sg = _norm_inproj(y1s, norm_pre_ret, w_in_r, segs_rs, tm=1024, tn=512,
                                  cos=cos_s, sin=sin_s, rotate_scale=rs, name="inproj_ret_sample")

    gated_rp, state_p = _retention(logg, r3(rq, b), r3(rk, b), r3(rv, b), r3(rg, b), None,
                                   chunk=256, out_dtype=BF16)
    gated_rs, state_s = _retention(logg, r3(sq, db), r3(sk, db), r3(sv, db), r3(sg, db),
                                   state_ret.astype(F32), chunk=t, out_dtype=F32)

    y2p = _outproj_norm_residual(gated_rp.reshape(b * s, wv_r), w_out_r, y1p, norm_post_ret,
                                 tm=512, tk=2048, name="outproj_ret_prompt")
    y2s = _outproj_norm_residual(gated_rs.reshape(db * t, wv_r), w_out_r, y1s, norm_post_ret,
                                 tm=512, tk=2048, name="outproj_ret_sample")

    return (y2p.reshape(b, s, d), y2s.reshape(db, t, d),
            kp.reshape(b, s, nh_a, HEAD_W), vp.reshape(b, s, nh_a, HEAD_W), state_p,
            ks.reshape(db, t, nh_a, HEAD_W), vs.reshape(db, t, nh_a, HEAD_W), state_s)
```

```python
import functools
import math

import jax
import jax.numpy as jnp
from jax import lax
from jax.experimental import pallas as pl
from jax.experimental.pallas import tpu as pltpu

F32 = jnp.float32
BF16 = jnp.bfloat16

NORM_EPS = 1e-6
LAMBDA_INIT = 0.8 - 0.6 * math.exp(-0.3 * 0)
ROPE_BASE = 10000.0
LOG2E = math.log2(math.e)
HEAD_W = 256
HALF = 128
BF16_ROWS = 16
VMEM_LIMIT_BYTES = 56 * 1024 * 1024

_SMEM = pl.BlockSpec(memory_space=pltpu.SMEM)


def _cparams(sem):
    return pltpu.CompilerParams(dimension_semantics=sem, vmem_limit_bytes=VMEM_LIMIT_BYTES)


def _silu(g):
    return g * (1.0 / (1.0 + jnp.exp(-g)))


def _dot_nt(a, b):
    return lax.dot_general(a, b, (((1,), (1,)), ((), ())), preferred_element_type=F32)


def _dot_tn(a, b):
    return lax.dot_general(a, b, (((0,), (0,)), ((), ())), preferred_element_type=F32)


def _dot(a, b):
    return jnp.dot(a, b, preferred_element_type=F32)


def _pad_rows(x, n):
    if x.shape[0] == n:
        return x
    return jnp.concatenate([x, jnp.zeros((n - x.shape[0], x.shape[1]), x.dtype)], axis=0)


def _lane_tile(x, width):
    return jnp.concatenate([x] * (width // x.shape[1]), axis=-1)


def _seg_index_map(j0, ntiles):
    def index_map(i, j):
        return (i, jnp.minimum(jnp.maximum(j - j0, 0), ntiles - 1))
    return index_map


def _inproj_kernel(*refs, seg_tiles, seg_kinds, tn, rotate_scale):
    n_in = 5 if rotate_scale is not None else 3
    x_ref, nw_ref, w_ref = refs[:3]
    cos_ref = sin_ref = None
    if rotate_scale is not None:
        cos_ref, sin_ref = refs[3:5]
    h_ref = refs[-1]
    out_refs = refs[n_in:-1]
    j = pl.program_id(1)

    @pl.when(j == 0)
    def _():
        x = x_ref[...]
        ms = jnp.mean(x * x, axis=-1, keepdims=True)
        h_ref[...] = (x * lax.rsqrt(ms + NORM_EPS) * nw_ref[...]).astype(BF16)

    acc = _dot(h_ref[...], w_ref[...])

    def rotated(a, scale):
        cos = cos_ref[...]
        sin = sin_ref[...]
        parts = []
        for hh in range(tn // HEAD_W):
            x1 = a[:, hh * HEAD_W: hh * HEAD_W + HALF]
            x2 = a[:, hh * HEAD_W + HALF: (hh + 1) * HEAD_W]
            parts.append((x1 * cos - x2 * sin) * scale)
            parts.append((x1 * sin + x2 * cos) * scale)
        return jnp.concatenate(parts, axis=-1)

    j0 = 0
    oi = 0
    for ntiles, kinds in zip(seg_tiles, seg_kinds):
        outs = []
        for kind in kinds:
            outs.append((out_refs[oi], kind))
            oi += 1

        @pl.when((j >= j0) & (j < j0 + ntiles))
        def _(outs=outs):
            for ref, kind in outs:
                if kind == "rot_q":
                    ref[...] = rotated(acc, rotate_scale[0]).astype(ref.dtype)
                elif kind == "rot_k":
                    ref[...] = rotated(acc, rotate_scale[1]).astype(ref.dtype)
                else:
                    ref[...] = acc.astype(ref.dtype)

        j0 += ntiles


def _norm_inproj(x, nw, w, segs, *, tm, tn, cos=None, sin=None, rotate_scale=None, name):
    m, d = x.shape
    n = w.shape[1]
    tm = min(tm, m)
    seg_tiles = tuple(width // tn for width, _ in segs)
    seg_kinds = tuple(tuple(kind for _, kind in outs) for _, outs in segs)
    out_shapes = []
    out_specs = []
    j0 = 0
    for (width, outs), ntiles in zip(segs, seg_tiles):
        for dtype, _ in outs:
            out_shapes.append(jax.ShapeDtypeStruct((m, width), dtype))
            out_specs.append(pl.BlockSpec((tm, tn), _seg_index_map(j0, ntiles)))
        j0 += ntiles
    assert j0 * tn == n
    in_specs = [
        pl.BlockSpec((tm, d), lambda i, j: (i, 0)),
        pl.BlockSpec((1, d), lambda i, j: (0, 0)),
        pl.BlockSpec((d, tn), lambda i, j: (0, j)),
    ]
    args = [x, nw.reshape(1, d), w]
    if rotate_scale is not None:
        nblk = cos.shape[0] // tm
        in_specs += [pl.BlockSpec((tm, HALF), lambda i, j: (i % nblk, 0))] * 2
        args += [cos, sin]
    kern = functools.partial(_inproj_kernel, seg_tiles=seg_tiles, seg_kinds=seg_kinds, tn=tn,
                             rotate_scale=rotate_scale)
    return pl.pallas_call(
        kern,
        out_shape=out_shapes,
        grid=(m // tm, n // tn),
        in_specs=in_specs,
        out_specs=out_specs,
        scratch_shapes=[pltpu.VMEM((tm, d), BF16)],
        compiler_params=_cparams(("parallel", "arbitrary")),
        name=name,
    )(*args)


def _outproj_kernel(a_ref, w_ref, x_ref, nw_ref, o_ref, acc_ref):
    kk = pl.program_id(1)

    @pl.when(kk == 0)
    def _():
        acc_ref[...] = jnp.zeros_like(acc_ref)

    acc_ref[...] += _dot(a_ref[...].astype(BF16), w_ref[...])

    @pl.when(kk == pl.num_programs(1) - 1)
    def _():
        z = acc_ref[...]
        ms = jnp.mean(z * z, axis=-1, keepdims=True)
        o_ref[...] = x_ref[...] + z * lax.rsqrt(ms + NORM_EPS) * nw_ref[...]


def _outproj_norm_residual(a, w, x, nw, *, tm, tk, name):
    m, kd = a.shape
    d = w.shape[1]
    tm = min(tm, m)
    return pl.pallas_call(
        _outproj_kernel,
        out_shape=jax.ShapeDtypeStruct((m, d), F32),
        grid=(m // tm, kd // tk),
        in_specs=[
            pl.BlockSpec((tm, tk), lambda i, k: (i, k)),
            pl.BlockSpec((tk, d), lambda i, k: (k, 0)),
            pl.BlockSpec((tm, d), lambda i, k: (i, 0)),
            pl.BlockSpec((1, d), lambda i, k: (0, 0)),
        ],
        out_specs=pl.BlockSpec((tm, d), lambda i, k: (i, 0)),
        scratch_shapes=[pltpu.VMEM((tm, d), F32)],
        compiler_params=_cparams(("parallel", "arbitrary")),
        name=name,
    )(a, w, x, nw.reshape(1, d))


def _diff_lambda(lq1_ref, lk1_ref, lq2_ref, lk2_ref):
    a = jnp.sum(lq1_ref[...] * lk1_ref[...], axis=-1, keepdims=True)
    b = jnp.sum(lq2_ref[...] * lk2_ref[...], axis=-1, keepdims=True)
    return jnp.exp(a) - jnp.exp(b) + LAMBDA_INIT


def _subln_gate(o, g, subln):
    ms = jnp.mean(o * o, axis=-1, keepdims=True)
    of = o * lax.rsqrt(ms + NORM_EPS) * subln * (1.0 - LAMBDA_INIT)
    return of * _silu(g)


def _attn_prompt_kernel(slopes_ref, q_ref, k_ref, v_ref, g_ref, subln_ref,
                        lq1_ref, lk1_ref, lq2_ref, lk2_ref, o_ref, m_ref, l_ref, acc_ref, *, tq, tk):
    h = pl.program_id(1)
    qi = pl.program_id(2)
    slope2 = slopes_ref[h] * LOG2E
    scale2 = HALF ** -0.5 * LOG2E
    q0 = qi * tq
    m_ref[...] = jnp.full(m_ref.shape, -jnp.inf, F32)
    l_ref[...] = jnp.zeros(l_ref.shape, F32)
    acc_ref[...] = jnp.zeros(acc_ref.shape, F32)

    def kv_step(j, carry, masked):
        k0 = pl.multiple_of(j * tk, tk)
        vblk = v_ref[0, pl.ds(k0, tk), :]
        col = lax.broadcasted_iota(jnp.int32, (1, tk), 1) + (k0 - q0)
        bias = slope2 * col.astype(F32)
        if masked:
            row = lax.broadcasted_iota(jnp.int32, (tq, 1), 0)
            valid = row >= col
        for c in range(2):
            qc = q_ref[0, :, c * HALF:(c + 1) * HALF]
            kc = k_ref[0, pl.ds(k0, tk), c * HALF:(c + 1) * HALF]
            s = _dot_nt(qc, kc) * scale2 + bias
            if masked:
                s = jnp.where(valid, s, -jnp.inf)
            m_old = m_ref[c]
            m_new = jnp.maximum(m_old, jnp.max(s, axis=-1, keepdims=True))
            alpha = jnp.exp2(m_old - m_new)
            p = jnp.exp2(s - _lane_tile(m_new, tk))
            m_ref[c] = m_new
            l_ref[c] = alpha * l_ref[c] + jnp.sum(p, axis=-1, keepdims=True)
            acc_ref[c] = _lane_tile(alpha, HEAD_W) * acc_ref[c] + _dot(p.astype(BF16), vblk)
        return carry

    n_full = qi * (tq // tk)
    lax.fori_loop(0, n_full, functools.partial(kv_step, masked=False), 0)
    for jj in range(tq // tk):
        kv_step(n_full + jj, 0, masked=True)

    lam = _diff_lambda(lq1_ref, lk1_ref, lq2_ref, lk2_ref)
    o = (acc_ref[0] / _lane_tile(l_ref[0], HEAD_W)
         - lam * (acc_ref[1] / _lane_tile(l_ref[1], HEAD_W)))
    o_ref[0] = _subln_gate(o, g_ref[0], subln_ref[...]).astype(o_ref.dtype)


def _attn_prompt(slopes, q, kb, vb, g, subln, lams, *, tq, tk):
    b, s, w = q.shape
    nh = w // HEAD_W
    blk_q = pl.BlockSpec((1, tq, HEAD_W), lambda bi, hi, qi: (bi, qi, hi))
    blk_kv = pl.BlockSpec((1, s, HEAD_W), lambda bi, hi, qi: (bi, 0, hi))
    vec = lambda n: pl.BlockSpec((1, n), lambda bi, hi, qi: (0, 0))
    kern = functools.partial(_attn_prompt_kernel, tq=tq, tk=tk)
    return pl.pallas_call(
        kern,
        out_shape=jax.ShapeDtypeStruct((b, s, w), BF16),
        grid=(b, nh, s // tq),
        in_specs=[_SMEM, blk_q, blk_kv, blk_kv, blk_q, vec(HEAD_W)] + [vec(HALF)] * 4,
        out_specs=blk_q,
        scratch_shapes=[pltpu.VMEM((2, tq, HALF), F32), pltpu.VMEM((2, tq, HALF), F32),
                        pltpu.VMEM((2, tq, HEAD_W), F32)],
        compiler_params=_cparams(("parallel", "parallel", "arbitrary")),
        name="diff_attn_prompt",
    )(slopes, q, kb, vb, g, subln.reshape(1, HEAD_W), *[v.reshape(1, HALF) for v in lams])


def _attn_sample_kernel(pt_ref, slopes_ref, q_ref, knew_ref, vnew_ref, g_ref, subln_ref,
                        lq1_ref, lk1_ref, lq2_ref, lk2_ref, *rest,
                        pages_per_step, n_steps, nh, t, page, past):
    del pt_ref
    pp = pages_per_step
    k_refs = rest[:pp]
    v_refs = rest[pp:2 * pp]
    o_ref = rest[2 * pp]
    wq_ref, bias_ref, sl_ref, m_ref, l_ref, acc_ref = rest[2 * pp + 1:]
    step = pl.program_id(1)
    scale = HALF ** -0.5
    nrow = nh * 2 * t
    lanes = page * nh
    log2 = lambda n: n.bit_length() - 1

    def row_head_tok():
        r = lax.broadcasted_iota(jnp.int32, (nrow, 1), 0)
        return jnp.right_shift(r, log2(2 * t)), jnp.bitwise_and(r, t - 1)

    def col_key_head(n):
        ln = lax.broadcasted_iota(jnp.int32, (1, n), 1)
        return jnp.right_shift(ln, log2(nh)), jnp.bitwise_and(ln, nh - 1)

    def slope_rows():
        rh, _ = row_head_tok()
        sl = jnp.zeros((nrow, 1), F32)
        for h in range(nh):
            sl = jnp.where(rh == h, slopes_ref[h], sl)
        return sl

    def alibi(dist_key0, n, causal):
        rh, rt = row_head_tok()
        key, ch = col_key_head(n)
        dist = dist_key0 + rt - key
        ok = ch == rh
        if causal:
            ok = ok & (dist >= 0)
        return jnp.where(ok, -slope_rows() * dist.astype(F32), -jnp.inf)

    @pl.when(step == 0)
    def _():
        q = q_ref[...]
        z = jnp.zeros((t, HALF), F32)
        for h in range(nh):
            q1 = q[:, h * HEAD_W: h * HEAD_W + HALF]
            q2 = q[:, h * HEAD_W + HALF: (h + 1) * HEAD_W]
            top = jnp.concatenate([q1, z], axis=-1)
            bot = jnp.concatenate([z, q2], axis=-1)
            wq_ref[h * 2 * t:(h + 1) * 2 * t, :] = jnp.concatenate([top, bot], axis=0).astype(BF16)
        bias_ref[...] = alibi(past, lanes, causal=False)
        sl_ref[...] = jnp.broadcast_to(slope_rows(), sl_ref.shape)
        m_ref[...] = jnp.full(m_ref.shape, -jnp.inf, F32)
        l_ref[...] = jnp.zeros(l_ref.shape, F32)
        acc_ref[...] = jnp.zeros(acc_ref.shape, F32)

    def update(k2, v2, bias, row_shift):
        st = _dot_nt(wq_ref[...], k2.astype(BF16)) * scale + bias
        m_old = m_ref[...]
        m_new = jnp.maximum(m_old, jnp.max(st, axis=-1, keepdims=True) + row_shift)
        alpha = jnp.exp(m_old - m_new)
        p = jnp.exp(st + _lane_tile(row_shift - m_new, st.shape[1]))
        m_ref[...] = m_new
        l_ref[...] = alpha * l_ref[...] + jnp.sum(p, axis=-1, keepdims=True)
        acc_ref[...] = (_lane_tile(alpha, HEAD_W) * acc_ref[...]
                        + _dot(p.astype(BF16), v2.astype(BF16)))

    for r in range(pp):
        pg = step * pp + r
        update(k_refs[r][...], v_refs[r][...], bias_ref[...],
               sl_ref[...] * (pg * page).astype(F32))

    @pl.when(step == n_steps - 1)
    def _():
        n_new = HALF
        update(_pad_rows(knew_ref[...], n_new), _pad_rows(vnew_ref[...], n_new),
               alibi(0, n_new, causal=True), jnp.zeros(sl_ref.shape, F32))
        lam = _diff_lambda(lq1_ref, lk1_ref, lq2_ref, lk2_ref)
        on = acc_ref[...] / _lane_tile(l_ref[...], HEAD_W)
        g = g_ref[...]
        for h in range(nh):
            o = on[h * 2 * t: h * 2 * t + t] - lam * on[h * 2 * t + t: (h + 1) * 2 * t]
            o_ref[:, h * HEAD_W:(h + 1) * HEAD_W] = _subln_gate(
                o, g[:, h * HEAD_W:(h + 1) * HEAD_W], subln_ref[...])


def _attn_sample(page_table, slopes, q, k_new, v_new, g, cache_k, cache_v, subln, lams,
                 *, pages_per_step):
    db, t, w = q.shape
    nh = w // HEAD_W
    n_phys, page = cache_k.shape[:2]
    n_pages = page_table.shape[1]
    past = n_pages * page
    pp = pages_per_step
    n_steps = n_pages // pp
    nrow = nh * 2 * t
    assert n_steps * pp == n_pages and nrow % BF16_ROWS == 0
    assert nh & (nh - 1) == 0 and t & (t - 1) == 0 and t * nh <= HALF
    ck = cache_k.reshape(n_phys, page * nh, HEAD_W)
    cv = cache_v.reshape(n_phys, page * nh, HEAD_W)
    kn = k_new.reshape(db, t * nh, HEAD_W)
    vn = v_new.reshape(db, t * nh, HEAD_W)

    def page_map(r):
        return lambda b, s, pt: (pt[b, s * pp + r], 0, 0)

    per_b = lambda shape: pl.BlockSpec((None,) + shape, lambda b, s, pt: (b, 0, 0))
    vec = lambda n: pl.BlockSpec((1, n), lambda b, s, pt: (0, 0))
    page_blk = (None, page * nh, HEAD_W)
    in_specs = ([_SMEM, per_b((t, w)), per_b((t * nh, HEAD_W)), per_b((t * nh, HEAD_W)),
                 per_b((t, w)), vec(HEAD_W)] + [vec(HALF)] * 4
                + [pl.BlockSpec(page_blk, page_map(r)) for r in range(pp)] * 2)
    kern = functools.partial(_attn_sample_kernel, pages_per_step=pp, n_steps=n_steps, nh=nh, t=t,
                             page=page, past=past)
    return pl.pallas_call(
        kern,
        out_shape=jax.ShapeDtypeStruct((db, t, w), F32),
        grid_spec=pltpu.PrefetchScalarGridSpec(
            num_scalar_prefetch=1,
            grid=(db, n_steps),
            in_specs=in_specs,
            out_specs=per_b((t, w)),
            scratch_shapes=[
                pltpu.VMEM((nrow, HEAD_W), BF16),
                pltpu.VMEM((nrow, page * nh), F32),
                pltpu.VMEM((nrow, HALF), F32),
                pltpu.VMEM((nrow, HALF), F32),
                pltpu.VMEM((nrow, HALF), F32),
                pltpu.VMEM((nrow, HEAD_W), F32),
            ],
        ),
        compiler_params=_cparams(("parallel", "arbitrary")),
        name="diff_attn_sample",
    )(page_table, slopes, q, kn, vn, g, subln.reshape(1, HEAD_W),
      *[v.reshape(1, HALF) for v in lams], *([ck] * pp), *([cv] * pp))


def _retention_kernel(logg_ref, q_ref, k_ref, v_ref, g_ref, *rest, c_true, has_init):
    if has_init:
        s0_ref, o_ref, s_ref = rest
    else:
        s0_ref = None
        o_ref, s_ref = rest
    h = pl.program_id(1)
    ci = pl.program_id(2)
    lg = jnp.full((1, 1), logg_ref[h], F32)
    c = max(c_true, BF16_ROWS)

    @pl.when(ci == 0)
    def _():
        if has_init:
            s_ref[0, 0] = s0_ref[0, 0]
        else:
            s_ref[0, 0] = jnp.zeros(s_ref.shape[2:], F32)

    q = _pad_rows(q_ref[0], c).astype(BF16)
    kf = _pad_rows(k_ref[0], c).astype(F32)
    v = _pad_rows(v_ref[0], c).astype(BF16)
    state = s_ref[0, 0]

    ri = lax.broadcasted_iota(jnp.int32, (c, 1), 0)
    cj = lax.broadcasted_iota(jnp.int32, (1, c), 1)
    diff = ri - cj
    decay = jnp.where(diff >= 0, jnp.exp(lg * jnp.maximum(diff, 0).astype(F32)), 0.0)
    rif = ri.astype(F32)
    scores = _dot_nt(q, kf.astype(BF16)) * decay
    o = _dot(scores.astype(BF16), v) + _dot(q, state.astype(BF16)) * jnp.exp(lg * (rif + 1.0))
    k_dec = (kf * jnp.exp(lg * (c_true - 1.0 - rif))).astype(BF16)
    s_ref[0, 0] = jnp.exp(lg * float(c_true)) * state + _dot_tn(k_dec, v)

    o = o[:c_true]
    ms = jnp.mean(o * o, axis=-1, keepdims=True)
    o_ref[0] = (o * lax.rsqrt(ms + NORM_EPS) * _silu(g_ref[0])).astype(o_ref.dtype)


def _retention(logg, q, k, v, g, s0, *, chunk, out_dtype):
    b, s, wqk = q.shape
    nh = logg.shape[0]
    dk = wqk // nh
    dv = v.shape[2] // nh
    blk = lambda wd: pl.BlockSpec((1, chunk, wd), lambda bi, hi, ci: (bi, ci, hi))
    st_blk = pl.BlockSpec((1, 1, dk, dv), lambda bi, hi, ci: (bi, hi, 0, 0))
    in_specs = [_SMEM, blk(dk), blk(dk), blk(dv), blk(dv)]
    args = [logg, q, k, v, g]
    if s0 is not None:
        in_specs.append(st_blk)
        args.append(s0)
    kern = functools.partial(_retention_kernel, c_true=chunk, has_init=s0 is not None)
    return pl.pallas_call(
        kern,
        out_shape=[jax.ShapeDtypeStruct((b, s, nh * dv), out_dtype),
                   jax.ShapeDtypeStruct((b, nh, dk, dv), F32)],
        grid=(b, nh, s // chunk),
        in_specs=in_specs,
        out_specs=[blk(dv), st_blk],
        compiler_params=_cparams(("parallel", "parallel", "arbitrary")),
        name="retention",
    )(*args)


def _rope_tables(pos, half):
    inv_freq = 1.0 / (ROPE_BASE ** jnp.linspace(0.0, 1.0, half, dtype=F32))
    ang = pos.astype(F32)[:, None] * inv_freq[None, :]
    return jnp.cos(ang), jnp.sin(ang)


def kernel(x_prompt, x_sample, cache_k, cache_v, page_table, state_ret, norm_pre_attn, w_in_attn, lambda_q1, lambda_k1, lambda_q2, lambda_k2, subln_w_attn, w_out_attn, norm_post_attn, norm_pre_ret, w_in_ret, w_out_ret, norm_post_ret):
    b, s, d = x_prompt.shape
    db, t, _ = x_sample.shape
    nh_a = cache_k.shape[2]
    nh_r = state_ret.shape[1]
    dk_r, dv_r = state_ret.shape[2:]
    past = page_table.shape[1] * cache_k.shape[1]
    wa = nh_a * HEAD_W
    wqk_r, wv_r = nh_r * dk_r, nh_r * dv_r

    slopes = jnp.exp2(-8.0 * jnp.arange(1, nh_a + 1, dtype=F32) / nh_a)
    logg = jnp.log1p(-jnp.exp2(-5.0 - jnp.arange(nh_r, dtype=F32)))
    lams = (lambda_q1, lambda_k1, lambda_q2, lambda_k2)
    w_in_a = w_in_attn.astype(BF16)
    w_out_a = w_out_attn.astype(BF16)
    w_in_r = w_in_ret.astype(BF16)
    w_out_r = w_out_ret.astype(BF16)

    xp = x_prompt.reshape(b * s, d)
    xs = x_sample.reshape(db * t, d)

    segs_p = [(wa, [(BF16, "q")]), (wa, [(F32, "k"), (BF16, "kb")]),
              (wa, [(F32, "v"), (BF16, "vb")]), (wa, [(F32, "g")])]
    qp, kp, kbp, vp, vbp, gp = _norm_inproj(xp, norm_pre_attn, w_in_a, segs_p, tm=1024, tn=512,
                                            name="inproj_attn_prompt")
    segs_s = [(wa, [(F32, "q")]), (wa, [(F32, "k")]), (wa, [(F32, "v")]), (wa, [(F32, "g")])]
    qs, ks, vs, gs = _norm_inproj(xs, norm_pre_attn, w_in_a, segs_s, tm=1024, tn=512,
                                  name="inproj_attn_sample")

    r3 = lambda a, n: a.reshape(n, -1, a.shape[-1])
    gated_p = _attn_prompt(slopes, r3(qp, b), r3(kbp, b), r3(vbp, b), r3(gp, b), subln_w_attn, lams,
                           tq=512, tk=512)
    gated_s = _attn_sample(page_table, slopes, r3(qs, db), ks.reshape(db, t, nh_a, HEAD_W),
                           vs.reshape(db, t, nh_a, HEAD_W), r3(gs, db), cache_k, cache_v,
                           subln_w_attn, lams, pages_per_step=4)

    y1p = _outproj_norm_residual(gated_p.reshape(b * s, wa), w_out_a, xp, norm_post_attn,
                                 tm=512, tk=2048, name="outproj_attn_prompt")
    y1s = _outproj_norm_residual(gated_s.reshape(db * t, wa), w_out_a, xs, norm_post_attn,
                                 tm=512, tk=2048, name="outproj_attn_sample")

    cos_p, sin_p = _rope_tables(jnp.arange(s, dtype=jnp.int32), dk_r // 2)
    cos_s, sin_s = _rope_tables(past + jnp.arange(t, dtype=jnp.int32), dk_r // 2)
    cos_s, sin_s = jnp.tile(cos_s, (db, 1)), jnp.tile(sin_s, (db, 1))
    rs = (1.0, dk_r ** -0.5)
    segs_rp = [(wqk_r, [(BF16, "rot_q")]), (wqk_r, [(BF16, "rot_k")]),
               (wv_r, [(BF16, "v")]), (wv_r, [(F32, "g")])]
    rq, rk, rv, rg = _norm_inproj(y1p, norm_pre_ret, w_in_r, segs_rp, tm=1024, tn=512,
                                  cos=cos_p, sin=sin_p, rotate_scale=rs, name="inproj_ret_prompt")
    segs_rs = [(wqk_r, [(F32, "rot_q")]), (wqk_r, [(F32, "rot_k")]),
               (wv_r, [(F32, "v")]), (wv_r, [(F32, "g")])]
    sq, sk, sv, sg = _norm_inproj(y1s, norm_pre_ret, w_in_r, segs_rs, tm=1024, tn=512,
                                  cos=cos_s, sin=sin_s, rotate_scale=rs, name="inproj_ret_sample")

    gated_rp, state_p = _retention(logg, r3(rq, b), r3(rk, b), r3(rv, b), r3(rg, b), None,
                                   chunk=256, out_dtype=BF16)
    gated_rs, state_s = _retention(logg, r3(sq, db), r3(sk, db), r3(sv, db), r3(sg, db),
                                   state_ret.astype(F32), chunk=t, out_dtype=F32)

    y2p = _outproj_norm_residual(gated_rp.reshape(b * s, wv_r), w_out_r, y1p, norm_post_ret,
                                 tm=512, tk=2048, name="outproj_ret_prompt")
    y2s = _outproj_norm_residual(gated_rs.reshape(db * t, wv_r), w_out_r, y1s, norm_post_ret,
                                 tm=512, tk=2048, name="outproj_ret_sample")

    return (y2p.reshape(b, s, d), y2s.reshape(db, t, d),
            kp.reshape(b, s, nh_a, HEAD_W), vp.reshape(b, s, nh_a, HEAD_W), state_p,
            ks.reshape(db, t, nh_a, HEAD_W), vs.reshape(db, t, nh_a, HEAD_W), state_s)
```

```python
import functools
import math

import jax
import jax.numpy as jnp
from jax import lax
from jax.experimental import pallas as pl
from jax.experimental.pallas import tpu as pltpu

F32 = jnp.float32
BF16 = jnp.bfloat16

NORM_EPS = 1e-6
LAMBDA_INIT = 0.8 - 0.6 * math.exp(-0.3 * 0)
ROPE_BASE = 10000.0
LOG2E = math.log2(math.e)
HEAD_W = 256
HALF = 128
BF16_ROWS = 16
VMEM_LIMIT_BYTES = 56 * 1024 * 1024

_SMEM = pl.BlockSpec(memory_space=pltpu.SMEM)


def _cparams(sem):
    return pltpu.CompilerParams(dimension_semantics=sem, vmem_limit_bytes=VMEM_LIMIT_BYTES)


def _silu(g):
    return g * (1.0 / (1.0 + jnp.exp(-g)))


def _dot_nt(a, b):
    return lax.dot_general(a, b, (((1,), (1,)), ((), ())), preferred_element_type=F32)


def _dot_tn(a, b):
    return lax.dot_general(a, b, (((0,), (0,)), ((), ())), preferred_element_type=F32)


def _dot(a, b):
    return jnp.dot(a, b, preferred_element_type=F32)


def _pad_rows(x, n):
    if x.shape[0] == n:
        return x
    return jnp.concatenate([x, jnp.zeros((n - x.shape[0], x.shape[1]), x.dtype)], axis=0)


def _lane_tile(x, width):
    return jnp.concatenate([x] * (width // x.shape[1]), axis=-1)


def _inproj_kernel(*refs, seg_kinds, rotate_scale):
    nseg = len(seg_kinds)
    x_ref, nw_ref = refs[:2]
    w_refs = refs[2:2 + nseg]
    n_in = 2 + nseg
    cos_ref = sin_ref = None
    if rotate_scale is not None:
        cos_ref, sin_ref = refs[n_in:n_in + 2]
        n_in += 2
    h_ref = refs[-1]
    out_refs = refs[n_in:-1]

    @pl.when(pl.program_id(1) == 0)
    def _():
        x = x_ref[...]
        ms = jnp.mean(x * x, axis=-1, keepdims=True)
        h_ref[...] = (x * lax.rsqrt(ms + NORM_EPS) * nw_ref[...]).astype(BF16)

    def rotated(a, scale):
        cos = cos_ref[...]
        sin = sin_ref[...]
        parts = []
        for hh in range(a.shape[1] // HEAD_W):
            x1 = a[:, hh * HEAD_W: hh * HEAD_W + HALF]
            x2 = a[:, hh * HEAD_W + HALF: (hh + 1) * HEAD_W]
            parts.append((x1 * cos - x2 * sin) * scale)
            parts.append((x1 * sin + x2 * cos) * scale)
        return jnp.concatenate(parts, axis=-1)

    oi = 0
    for w_ref, kinds in zip(w_refs, seg_kinds):
        acc = _dot(h_ref[...], w_ref[...])
        for kind in kinds:
            ref = out_refs[oi]
            oi += 1
            if kind == "rot_q":
                ref[...] = rotated(acc, rotate_scale[0]).astype(ref.dtype)
            elif kind == "rot_k":
                ref[...] = rotated(acc, rotate_scale[1]).astype(ref.dtype)
            else:
                ref[...] = acc.astype(ref.dtype)


def _norm_inproj(x, nw, w, segs, *, tm, n_col_tiles, cos=None, sin=None, rotate_scale=None, name):
    m, d = x.shape
    tm = min(tm, m)
    nj = n_col_tiles
    seg_kinds = tuple(tuple(kind for _, kind in outs) for _, outs in segs)
    in_specs = [
        pl.BlockSpec((tm, d), lambda i, j: (i, 0)),
        pl.BlockSpec((1, d), lambda i, j: (0, 0)),
    ]
    args = [x, nw.reshape(1, d)]
    out_shapes = []
    out_specs = []
    off = 0
    for width, outs in segs:
        tw = width // nj
        assert tw * nj == width and off % tw == 0 and tw % HEAD_W == 0
        in_specs.append(pl.BlockSpec((d, tw), lambda i, j, o=off // tw: (0, o + j)))
        args.append(w)
        for dtype, _ in outs:
            out_shapes.append(jax.ShapeDtypeStruct((m, width), dtype))
            out_specs.append(pl.BlockSpec((tm, tw), lambda i, j: (i, j)))
        off += width
    assert off == w.shape[1]
    if rotate_scale is not None:
        nblk = cos.shape[0] // tm
        in_specs += [pl.BlockSpec((tm, HALF), lambda i, j: (i % nblk, 0))] * 2
        args += [cos, sin]
    kern = functools.partial(_inproj_kernel, seg_kinds=seg_kinds, rotate_scale=rotate_scale)
    return pl.pallas_call(
        kern,
        out_shape=out_shapes,
        grid=(m // tm, nj),
        in_specs=in_specs,
        out_specs=out_specs,
        scratch_shapes=[pltpu.VMEM((tm, d), BF16)],
        compiler_params=_cparams(("parallel", "arbitrary")),
        name=name,
    )(*args)


def _outproj_kernel(a_ref, w_ref, x_ref, nw_ref, o_ref, *, tsub):
    for r0 in range(0, a_ref.shape[0], tsub):
        rows = pl.ds(r0, tsub)
        z = _dot(a_ref[rows, :].astype(BF16), w_ref[...])
        ms = jnp.mean(z * z, axis=-1, keepdims=True)
        o_ref[rows, :] = x_ref[rows, :] + z * lax.rsqrt(ms + NORM_EPS) * nw_ref[...]


def _outproj_norm_residual(a, w, x, nw, *, tm, tsub, name):
    m, kd = a.shape
    d = w.shape[1]
    tm = min(tm, m)
    tsub = min(tsub, tm)
    return pl.pallas_call(
        functools.partial(_outproj_kernel, tsub=tsub),
        out_shape=jax.ShapeDtypeStruct((m, d), F32),
        grid=(m // tm,),
        in_specs=[
            pl.BlockSpec((tm, kd), lambda i: (i, 0)),
            pl.BlockSpec((kd, d), lambda i: (0, 0), pipeline_mode=pl.Buffered(1)),
            pl.BlockSpec((tm, d), lambda i: (i, 0)),
            pl.BlockSpec((1, d), lambda i: (0, 0)),
        ],
        out_specs=pl.BlockSpec((tm, d), lambda i: (i, 0)),
        compiler_params=_cparams(("parallel",)),
        name=name,
    )(a, w, x, nw.reshape(1, d))


def _diff_lambda(lq1_ref, lk1_ref, lq2_ref, lk2_ref):
    a = jnp.sum(lq1_ref[...] * lk1_ref[...], axis=-1, keepdims=True)
    b = jnp.sum(lq2_ref[...] * lk2_ref[...], axis=-1, keepdims=True)
    return jnp.exp(a) - jnp.exp(b) + LAMBDA_INIT


def _subln_gate(o, g, subln):
    ms = jnp.mean(o * o, axis=-1, keepdims=True)
    of = o * lax.rsqrt(ms + NORM_EPS) * subln * (1.0 - LAMBDA_INIT)
    return of * _silu(g)


def _attn_prompt_kernel(slopes_ref, q_ref, k_ref, v_ref, g_ref, subln_ref,
                        lq1_ref, lk1_ref, lq2_ref, lk2_ref, o_ref, m_ref, l_ref, acc_ref,
                        *, tq, tk, tsub):
    h = pl.program_id(1)
    qi = pl.program_id(2)
    slope2 = slopes_ref[h] * LOG2E
    scale2 = HALF ** -0.5 * LOG2E
    q0 = qi * tq
    m_ref[...] = jnp.full(m_ref.shape, -jnp.inf, F32)
    l_ref[...] = jnp.zeros(l_ref.shape, F32)
    acc_ref[...] = jnp.zeros(acc_ref.shape, F32)

    def kv_step(j, carry, masked):
        k0 = pl.multiple_of(j * tk, tk)
        vblk = v_ref[0, pl.ds(k0, tk), :]
        col = lax.broadcasted_iota(jnp.int32, (1, tk), 1) + (k0 - q0)
        bias = slope2 * col.astype(F32)
        for c in range(2):
            kc = k_ref[0, pl.ds(k0, tk), c * HALF:(c + 1) * HALF]
            for r0 in range(0, tq, tsub):
                rows = pl.ds(r0, tsub)
                s = _dot_nt(q_ref[0, rows, c * HALF:(c + 1) * HALF], kc) * scale2 + bias
                if masked:
                    row = lax.broadcasted_iota(jnp.int32, (tsub, 1), 0) + r0
                    s = jnp.where(row >= col, s, -jnp.inf)
                m_old = m_ref[c, rows, :]
                m_new = jnp.maximum(m_old, jnp.max(s, axis=-1, keepdims=True))
                alpha = jnp.exp2(m_old - m_new)
                p = jnp.exp2(s - _lane_tile(m_new, tk))
                m_ref[c, rows, :] = m_new
                l_ref[c, rows, :] = alpha * l_ref[c, rows, :] + jnp.sum(p, axis=-1, keepdims=True)
                acc_ref[c, rows, :] = (_lane_tile(alpha, HEAD_W) * acc_ref[c, rows, :]
                                       + _dot(p.astype(BF16), vblk))
        return carry

    n_full = qi * (tq // tk)
    lax.fori_loop(0, n_full, functools.partial(kv_step, masked=False), 0)
    for jj in range(tq // tk):
        kv_step(n_full + jj, 0, masked=True)

    lam = _diff_lambda(lq1_ref, lk1_ref, lq2_ref, lk2_ref)
    o = (acc_ref[0] / _lane_tile(l_ref[0], HEAD_W)
         - lam * (acc_ref[1] / _lane_tile(l_ref[1], HEAD_W)))
    o_ref[0] = _subln_gate(o, g_ref[0], subln_ref[...]).astype(o_ref.dtype)


def _attn_prompt(slopes, q, kb, vb, g, subln, lams, *, tq, tk, tsub):
    b, s, w = q.shape
    nh = w // HEAD_W
    blk_q = pl.BlockSpec((1, tq, HEAD_W), lambda bi, hi, qi: (bi, qi, hi))
    blk_kv = pl.BlockSpec((1, s, HEAD_W), lambda bi, hi, qi: (bi, 0, hi))
    vec = lambda n: pl.BlockSpec((1, n), lambda bi, hi, qi: (0, 0))
    kern = functools.partial(_attn_prompt_kernel, tq=tq, tk=tk, tsub=tsub)
    return pl.pallas_call(
        kern,
        out_shape=jax.ShapeDtypeStruct((b, s, w), BF16),
        grid=(b, nh, s // tq),
        in_specs=[_SMEM, blk_q, blk_kv, blk_kv, blk_q, vec(HEAD_W)] + [vec(HALF)] * 4,
        out_specs=blk_q,
        scratch_shapes=[pltpu.VMEM((2, tq, HALF), F32), pltpu.VMEM((2, tq, HALF), F32),
                        pltpu.VMEM((2, tq, HEAD_W), F32)],
        compiler_params=_cparams(("parallel", "parallel", "arbitrary")),
        name="diff_attn_prompt",
    )(slopes, q, kb, vb, g, subln.reshape(1, HEAD_W), *[v.reshape(1, HALF) for v in lams])


def _attn_sample_kernel(pt_ref, slopes_ref, q_ref, knew_ref, vnew_ref, g_ref, subln_ref,
                        lq1_ref, lk1_ref, lq2_ref, lk2_ref, *rest,
                        pages_per_step, n_steps, nh, t, page, past):
    del pt_ref
    pp = pages_per_step
    k_refs = rest[:pp]
    v_refs = rest[pp:2 * pp]
    o_ref = rest[2 * pp]
    wq_ref, bias_ref, sl_ref, m_ref, l_ref, acc_ref = rest[2 * pp + 1:]
    step = pl.program_id(1)
    scale = HALF ** -0.5
    nrow = nh * 2 * t
    lanes = page * nh
    log2 = lambda n: n.bit_length() - 1

    def row_head_tok():
        r = lax.broadcasted_iota(jnp.int32, (nrow, 1), 0)
        return jnp.right_shift(r, log2(2 * t)), jnp.bitwise_and(r, t - 1)

    def col_key_head(n):
        ln = lax.broadcasted_iota(jnp.int32, (1, n), 1)
        return jnp.right_shift(ln, log2(nh)), jnp.bitwise_and(ln, nh - 1)

    def slope_rows():
        rh, _ = row_head_tok()
        sl = jnp.zeros((nrow, 1), F32)
        for h in range(nh):
            sl = jnp.where(rh == h, slopes_ref[h], sl)
        return sl

    def alibi(dist_key0, n, causal):
        rh, rt = row_head_tok()
        key, ch = col_key_head(n)
        dist = dist_key0 + rt - key
        ok = ch == rh
        if causal:
            ok = ok & (dist >= 0)
        return jnp.where(ok, -slope_rows() * dist.astype(F32), -jnp.inf)

    @pl.when(step == 0)
    def _():
        q = q_ref[...]
        z = jnp.zeros((t, HALF), F32)
        for h in range(nh):
            q1 = q[:, h * HEAD_W: h * HEAD_W + HALF]
            q2 = q[:, h * HEAD_W + HALF: (h + 1) * HEAD_W]
            top = jnp.concatenate([q1, z], axis=-1)
            bot = jnp.concatenate([z, q2], axis=-1)
            wq_ref[h * 2 * t:(h + 1) * 2 * t, :] = jnp.concatenate([top, bot], axis=0).astype(BF16)
        bias_ref[...] = alibi(past, lanes, causal=False)
        sl_ref[...] = jnp.broadcast_to(slope_rows(), sl_ref.shape)
        m_ref[...] = jnp.full(m_ref.shape, -jnp.inf, F32)
        l_ref[...] = jnp.zeros(l_ref.shape, F32)
        acc_ref[...] = jnp.zeros(acc_ref.shape, F32)

    def update(k2, v2, bias, row_shift):
        st = _dot_nt(wq_ref[...], k2.astype(BF16)) * scale + bias
        m_old = m_ref[...]
        m_new = jnp.maximum(m_old, jnp.max(st, axis=-1, keepdims=True) + row_shift)
        alpha = jnp.exp(m_old - m_new)
        p = jnp.exp(st + _lane_tile(row_shift - m_new, st.shape[1]))
        m_ref[...] = m_new
        l_ref[...] = alpha * l_ref[...] + jnp.sum(p, axis=-1, keepdims=True)
        acc_ref[...] = (_lane_tile(alpha, HEAD_W) * acc_ref[...]
                        + _dot(p.astype(BF16), v2.astype(BF16)))

    for r in range(pp):
        pg = step * pp + r
        update(k_refs[r][...], v_refs[r][...], bias_ref[...],
               sl_ref[...] * (pg * page).astype(F32))

    @pl.when(step == n_steps - 1)
    def _():
        n_new = HALF
        update(_pad_rows(knew_ref[...], n_new), _pad_rows(vnew_ref[...], n_new),
               alibi(0, n_new, causal=True), jnp.zeros(sl_ref.shape, F32))
        lam = _diff_lambda(lq1_ref, lk1_ref, lq2_ref, lk2_ref)
        on = acc_ref[...] / _lane_tile(l_ref[...], HEAD_W)
        g = g_ref[...]
        for h in range(nh):
            o = on[h * 2 * t: h * 2 * t + t] - lam * on[h * 2 * t + t: (h + 1) * 2 * t]
            o_ref[:, h * HEAD_W:(h + 1) * HEAD_W] = _subln_gate(
                o, g[:, h * HEAD_W:(h + 1) * HEAD_W], subln_ref[...])


def _attn_sample(page_table, slopes, q, k_new, v_new, g, cache_k, cache_v, subln, lams,
                 *, pages_per_step):
    db, t, w = q.shape
    nh = w // HEAD_W
    n_phys, page = cache_k.shape[:2]
    n_pages = page_table.shape[1]
    past = n_pages * page
    pp = pages_per_step
    n_steps = n_pages // pp
    nrow = nh * 2 * t
    assert n_steps * pp == n_pages and nrow % BF16_ROWS == 0
    assert nh & (nh - 1) == 0 and t & (t - 1) == 0 and t * nh <= HALF
    ck = cache_k.reshape(n_phys, page * nh, HEAD_W)
    cv = cache_v.reshape(n_phys, page * nh, HEAD_W)
    kn = k_new.reshape(db, t * nh, HEAD_W)
    vn = v_new.reshape(db, t * nh, HEAD_W)

    def page_map(r):
        return lambda b, s, pt: (pt[b, s * pp + r], 0, 0)

    per_b = lambda shape: pl.BlockSpec((None,) + shape, lambda b, s, pt: (b, 0, 0))
    vec = lambda n: pl.BlockSpec((1, n), lambda b, s, pt: (0, 0))
    page_blk = (None, page * nh, HEAD_W)
    in_specs = ([_SMEM, per_b((t, w)), per_b((t * nh, HEAD_W)), per_b((t * nh, HEAD_W)),
                 per_b((t, w)), vec(HEAD_W)] + [vec(HALF)] * 4
                + [pl.BlockSpec(page_blk, page_map(r)) for r in range(pp)] * 2)
    kern = functools.partial(_attn_sample_kernel, pages_per_step=pp, n_steps=n_steps, nh=nh, t=t,
                             page=page, past=past)
    return pl.pallas_call(
        kern,
        out_shape=jax.ShapeDtypeStruct((db, t, w), F32),
        grid_spec=pltpu.PrefetchScalarGridSpec(
            num_scalar_prefetch=1,
            grid=(db, n_steps),
            in_specs=in_specs,
            out_specs=per_b((t, w)),
            scratch_shapes=[
                pltpu.VMEM((nrow, HEAD_W), BF16),
                pltpu.VMEM((nrow, page * nh), F32),
                pltpu.VMEM((nrow, HALF), F32),
                pltpu.VMEM((nrow, HALF), F32),
                pltpu.VMEM((nrow, HALF), F32),
                pltpu.VMEM((nrow, HEAD_W), F32),
            ],
        ),
        compiler_params=_cparams(("parallel", "arbitrary")),
        name="diff_attn_sample",
    )(page_table, slopes, q, kn, vn, g, subln.reshape(1, HEAD_W),
      *[v.reshape(1, HALF) for v in lams], *([ck] * pp), *([cv] * pp))


def _retention_kernel(logg_ref, q_ref, k_ref, v_ref, g_ref, *rest, chunk, n_chunks, has_init):
    if has_init:
        s0_ref, o_ref, s_ref, decay_ref = rest
    else:
        s0_ref = None
        o_ref, s_ref, decay_ref = rest
    h = pl.program_id(1)
    lg = jnp.full((1, 1), logg_ref[h], F32)
    c = max(chunk, BF16_ROWS)
    ri = lax.broadcasted_iota(jnp.int32, (c, 1), 0)
    rif = ri.astype(F32)

    @pl.when(pl.program_id(2) == 0)
    def _():
        if has_init:
            s_ref[0, 0] = s0_ref[0, 0]
        else:
            s_ref[0, 0] = jnp.zeros(s_ref.shape[2:], F32)
        diff = ri - lax.broadcasted_iota(jnp.int32, (1, c), 1)
        decay_ref[...] = jnp.where(diff >= 0, jnp.exp(lg * jnp.maximum(diff, 0).astype(F32)), 0.0)

    d_query = jnp.exp(lg * (rif + 1.0))
    d_key = jnp.exp(lg * (chunk - 1.0 - rif))
    d_chunk = jnp.exp(lg * float(chunk))
    state = s_ref[0, 0]
    for n in range(n_chunks):
        rows = pl.ds(n * chunk, chunk)
        q = _pad_rows(q_ref[0, rows, :], c).astype(BF16)
        kf = _pad_rows(k_ref[0, rows, :], c).astype(F32)
        v = _pad_rows(v_ref[0, rows, :], c).astype(BF16)
        scores = _dot_nt(q, kf.astype(BF16)) * decay_ref[...]
        o = _dot(scores.astype(BF16), v) + _dot(q, state.astype(BF16)) * d_query
        state = d_chunk * state + _dot_tn((kf * d_key).astype(BF16), v)
        o = o[:chunk]
        ms = jnp.mean(o * o, axis=-1, keepdims=True)
        o_ref[0, rows, :] = (o * lax.rsqrt(ms + NORM_EPS) * _silu(g_ref[0, rows, :])).astype(o_ref.dtype)
    s_ref[0, 0] = state


def _retention(logg, q, k, v, g, s0, *, chunk, chunks_per_step, out_dtype):
    b, s, wqk = q.shape
    nh = logg.shape[0]
    dk = wqk // nh
    dv = v.shape[2] // nh
    rows = chunk * chunks_per_step
    blk = lambda wd: pl.BlockSpec((1, rows, wd), lambda bi, hi, ci: (bi, ci, hi))
    st_blk = pl.BlockSpec((1, 1, dk, dv), lambda bi, hi, ci: (bi, hi, 0, 0))
    in_specs = [_SMEM, blk(dk), blk(dk), blk(dv), blk(dv)]
    args = [logg, q, k, v, g]
    if s0 is not None:
        in_specs.append(st_blk)
        args.append(s0)
    kern = functools.partial(_retention_kernel, chunk=chunk, n_chunks=chunks_per_step,
                             has_init=s0 is not None)
    c = max(chunk, BF16_ROWS)
    return pl.pallas_call(
        kern,
        out_shape=[jax.ShapeDtypeStruct((b, s, nh * dv), out_dtype),
                   jax.ShapeDtypeStruct((b, nh, dk, dv), F32)],
        grid=(b, nh, s // rows),
        in_specs=in_specs,
        out_specs=[blk(dv), st_blk],
        scratch_shapes=[pltpu.VMEM((c, c), F32)],
        compiler_params=_cparams(("parallel", "parallel", "arbitrary")),
        name="retention",
    )(*args)


def _rope_tables(pos, half):
    inv_freq = 1.0 / (ROPE_BASE ** jnp.linspace(0.0, 1.0, half, dtype=F32))
    ang = pos.astype(F32)[:, None] * inv_freq[None, :]
    return jnp.cos(ang), jnp.sin(ang)


def kernel(x_prompt, x_sample, cache_k, cache_v, page_table, state_ret, norm_pre_attn, w_in_attn, lambda_q1, lambda_k1, lambda_q2, lambda_k2, subln_w_attn, w_out_attn, norm_post_attn, norm_pre_ret, w_in_ret, w_out_ret, norm_post_ret):
    b, s, d = x_prompt.shape
    db, t, _ = x_sample.shape
    nh_a = cache_k.shape[2]
    nh_r = state_ret.shape[1]
    dk_r, dv_r = state_ret.shape[2:]
    past = page_table.shape[1] * cache_k.shape[1]
    wa = nh_a * HEAD_W
    wqk_r, wv_r = nh_r * dk_r, nh_r * dv_r

    slopes = jnp.exp2(-8.0 * jnp.arange(1, nh_a + 1, dtype=F32) / nh_a)
    logg = jnp.log1p(-jnp.exp2(-5.0 - jnp.arange(nh_r, dtype=F32)))
    lams = (lambda_q1, lambda_k1, lambda_q2, lambda_k2)
    w_in_a = w_in_attn.astype(BF16)
    w_out_a = w_out_attn.astype(BF16)
    w_in_r = w_in_ret.astype(BF16)
    w_out_r = w_out_ret.astype(BF16)

    xp = x_prompt.reshape(b * s, d)
    xs = x_sample.reshape(db * t, d)

    segs_p = [(wa, [(BF16, "q")]), (wa, [(F32, "k"), (BF16, "kb")]),
              (wa, [(F32, "v"), (BF16, "vb")]), (wa, [(F32, "g")])]
    qp, kp, kbp, vp, vbp, gp = _norm_inproj(xp, norm_pre_attn, w_in_a, segs_p, tm=1024, n_col_tiles=8,
                                            name="inproj_attn_prompt")
    segs_s = [(wa, [(F32, "q")]), (wa, [(F32, "k")]), (wa, [(F32, "v")]), (wa, [(F32, "g")])]
    qs, ks, vs, gs = _norm_inproj(xs, norm_pre_attn, w_in_a, segs_s, tm=1024, n_col_tiles=4,
                                  name="inproj_attn_sample")

    r3 = lambda a, n: a.reshape(n, -1, a.shape[-1])
    gated_p = _attn_prompt(slopes, r3(qp, b), r3(kbp, b), r3(vbp, b), r3(gp, b), subln_w_attn, lams,
                           tq=512, tk=512, tsub=128)
    gated_s = _attn_sample(page_table, slopes, r3(qs, db), ks.reshape(db, t, nh_a, HEAD_W),
                           vs.reshape(db, t, nh_a, HEAD_W), r3(gs, db), cache_k, cache_v,
                           subln_w_attn, lams, pages_per_step=4)

    y1p = _outproj_norm_residual(gated_p.reshape(b * s, wa), w_out_a, xp, norm_post_attn,
                                 tm=512, tsub=128, name="outproj_attn_prompt")
    y1s = _outproj_norm_residual(gated_s.reshape(db * t, wa), w_out_a, xs, norm_post_attn,
                                 tm=512, tsub=128, name="outproj_attn_sample")

    cos_p, sin_p = _rope_tables(jnp.arange(s, dtype=jnp.int32), dk_r // 2)
    cos_s, sin_s = _rope_tables(past + jnp.arange(t, dtype=jnp.int32), dk_r // 2)
    cos_s, sin_s = jnp.tile(cos_s, (db, 1)), jnp.tile(sin_s, (db, 1))
    rs = (1.0, dk_r ** -0.5)
    segs_rp = [(wqk_r, [(BF16, "rot_q")]), (wqk_r, [(BF16, "rot_k")]),
               (wv_r, [(BF16, "v")]), (wv_r, [(F32, "g")])]
    rq, rk, rv, rg = _norm_inproj(y1p, norm_pre_ret, w_in_r, segs_rp, tm=1024, n_col_tiles=8,
                                  cos=cos_p, sin=sin_p, rotate_scale=rs, name="inproj_ret_prompt")
    segs_rs = [(wqk_r, [(F32, "rot_q")]), (wqk_r, [(F32, "rot_k")]),
               (wv_r, [(F32, "v")]), (wv_r, [(F32, "g")])]
    sq, sk, sv, sg = _norm_inproj(y1s, norm_pre_ret, w_in_r, segs_rs, tm=1024, n_col_tiles=4,
                                  cos=cos_s, sin=sin_s, rotate_scale=rs, name="inproj_ret_sample")

    gated_rp, state_p = _retention(logg, r3(rq, b), r3(rk, b), r3(rv, b), r3(rg, b), None,
                                   chunk=256, chunks_per_step=4, out_dtype=BF16)
    gated_rs, state_s = _retention(logg, r3(sq, db), r3(sk, db), r3(sv, db), r3(sg, db),
                                   state_ret.astype(F32), chunk=t, chunks_per_step=1, out_dtype=F32)

    y2p = _outproj_norm_residual(gated_rp.reshape(b * s, wv_r), w_out_r, y1p, norm_post_ret,
                                 tm=512, tsub=128, name="outproj_ret_prompt")
    y2s = _outproj_norm_residual(gated_rs.reshape(db * t, wv_r), w_out_r, y1s, norm_post_ret,
                                 tm=512, tsub=128, name="outproj_ret_sample")

    return (y2p.reshape(b, s, d), y2s.reshape(db, t, d),
            kp.reshape(b, s, nh_a, HEAD_W), vp.reshape(b, s, nh_a, HEAD_W), state_p,
            ks.reshape(db, t, nh_a, HEAD_W), vs.reshape(db, t, nh_a, HEAD_W), state_s)
```

```python
import functools
import math

import jax
import jax.numpy as jnp
from jax import lax
from jax.experimental import pallas as pl
from jax.experimental.pallas import tpu as pltpu

F32 = jnp.float32
BF16 = jnp.bfloat16

NORM_EPS = 1e-6
LAMBDA_INIT = 0.8 - 0.6 * math.exp(-0.3 * 0)
ROPE_BASE = 10000.0
LOG2E = math.log2(math.e)
HEAD_W = 256
HALF = 128
BF16_ROWS = 16
VMEM_LIMIT_BYTES = 56 * 1024 * 1024

_SMEM = pl.BlockSpec(memory_space=pltpu.SMEM)


def _cparams(sem):
    return pltpu.CompilerParams(dimension_semantics=sem, vmem_limit_bytes=VMEM_LIMIT_BYTES)


def _silu(g):
    return g * (1.0 / (1.0 + jnp.exp(-g)))


def _dot_nt(a, b):
    return lax.dot_general(a, b, (((1,), (1,)), ((), ())), preferred_element_type=F32)


def _dot_tn(a, b):
    return lax.dot_general(a, b, (((0,), (0,)), ((), ())), preferred_element_type=F32)


def _dot(a, b):
    return jnp.dot(a, b, preferred_element_type=F32)


def _pad_rows(x, n):
    if x.shape[0] == n:
        return x
    return jnp.concatenate([x, jnp.zeros((n - x.shape[0], x.shape[1]), x.dtype)], axis=0)


def _lane_tile(x, width):
    return jnp.concatenate([x] * (width // x.shape[1]), axis=-1)


def _inproj_kernel(*refs, seg_kinds, rotate_scale, emit_bf16_weights):
    nseg = len(seg_kinds)
    x_ref, nw_ref = refs[:2]
    w_refs = refs[2:2 + nseg]
    n_in = 2 + nseg
    cos_ref = sin_ref = None
    if rotate_scale is not None:
        cos_ref, sin_ref = refs[n_in:n_in + 2]
        n_in += 2
    h_ref = refs[-1]
    out_refs = refs[n_in:-1]
    if emit_bf16_weights:
        out_refs, wb_refs = out_refs[:-nseg], out_refs[-nseg:]
    else:
        wb_refs = (None,) * nseg

    @pl.when(pl.program_id(1) == 0)
    def _():
        x = x_ref[...]
        ms = jnp.mean(x * x, axis=-1, keepdims=True)
        h_ref[...] = (x * lax.rsqrt(ms + NORM_EPS) * nw_ref[...]).astype(BF16)

    def rotated(a, scale):
        cos = cos_ref[...]
        sin = sin_ref[...]
        parts = []
        for hh in range(a.shape[1] // HEAD_W):
            x1 = a[:, hh * HEAD_W: hh * HEAD_W + HALF]
            x2 = a[:, hh * HEAD_W + HALF: (hh + 1) * HEAD_W]
            parts.append((x1 * cos - x2 * sin) * scale)
            parts.append((x1 * sin + x2 * cos) * scale)
        return jnp.concatenate(parts, axis=-1)

    oi = 0
    for w_ref, wb_ref, kinds in zip(w_refs, wb_refs, seg_kinds):
        w = w_ref[...]
        if wb_ref is not None:
            w = w.astype(BF16)
            wb_ref[...] = w
        acc = _dot(h_ref[...], w)
        for kind in kinds:
            ref = out_refs[oi]
            oi += 1
            if kind == "rot_q":
                ref[...] = rotated(acc, rotate_scale[0]).astype(ref.dtype)
            elif kind == "rot_k":
                ref[...] = rotated(acc, rotate_scale[1]).astype(ref.dtype)
            else:
                ref[...] = acc.astype(ref.dtype)


def _norm_inproj(x, nw, w, segs, *, tm, n_col_tiles, cos=None, sin=None, rotate_scale=None, name):
    m, d = x.shape
    tm = min(tm, m)
    nj = n_col_tiles
    emit = not isinstance(w, (list, tuple))
    assert not emit or m == tm
    seg_kinds = tuple(tuple(kind for _, kind in outs) for _, outs in segs)
    in_specs = [
        pl.BlockSpec((tm, d), lambda i, j: (i, 0)),
        pl.BlockSpec((1, d), lambda i, j: (0, 0)),
    ]
    args = [x, nw.reshape(1, d)]
    out_shapes = []
    out_specs = []
    wb_shapes = []
    wb_specs = []
    off = 0
    for si, (width, outs) in enumerate(segs):
        tw = width // nj
        assert tw * nj == width and off % tw == 0 and tw % HEAD_W == 0
        if emit:
            in_specs.append(pl.BlockSpec((d, tw), lambda i, j, o=off // tw: (0, o + j)))
            args.append(w)
            wb_shapes.append(jax.ShapeDtypeStruct((d, width), BF16))
            wb_specs.append(pl.BlockSpec((d, tw), lambda i, j: (0, j)))
        else:
            assert w[si].shape == (d, width)
            in_specs.append(pl.BlockSpec((d, tw), lambda i, j: (0, j)))
            args.append(w[si])
        for dtype, _ in outs:
            out_shapes.append(jax.ShapeDtypeStruct((m, width), dtype))
            out_specs.append(pl.BlockSpec((tm, tw), lambda i, j: (i, j)))
        off += width
    assert not emit or off == w.shape[1]
    if rotate_scale is not None:
        nblk = cos.shape[0] // tm
        in_specs += [pl.BlockSpec((tm, HALF), lambda i, j: (i % nblk, 0))] * 2
        args += [cos, sin]
    out_shapes += wb_shapes
    out_specs += wb_specs
    kern = functools.partial(_inproj_kernel, seg_kinds=seg_kinds, rotate_scale=rotate_scale,
                             emit_bf16_weights=emit)
    return pl.pallas_call(
        kern,
        out_shape=out_shapes,
        grid=(m // tm, nj),
        in_specs=in_specs,
        out_specs=out_specs,
        scratch_shapes=[pltpu.VMEM((tm, d), BF16)],
        compiler_params=_cparams(("parallel", "arbitrary")),
        name=name,
    )(*args)


def _outproj_kernel(a_ref, w_ref, x_ref, nw_ref, o_ref, *, tsub):
    for r0 in range(0, a_ref.shape[0], tsub):
        rows = pl.ds(r0, tsub)
        z = _dot(a_ref[rows, :].astype(BF16), w_ref[...])
        ms = jnp.mean(z * z, axis=-1, keepdims=True)
        o_ref[rows, :] = x_ref[rows, :] + z * lax.rsqrt(ms + NORM_EPS) * nw_ref[...]


def _outproj_norm_residual(a, w, x, nw, *, tm, tsub, name):
    m, kd = a.shape
    d = w.shape[1]
    tm = min(tm, m)
    tsub = min(tsub, tm)
    return pl.pallas_call(
        functools.partial(_outproj_kernel, tsub=tsub),
        out_shape=jax.ShapeDtypeStruct((m, d), F32),
        grid=(m // tm,),
        in_specs=[
            pl.BlockSpec((tm, kd), lambda i: (i, 0)),
            pl.BlockSpec((kd, d), lambda i: (0, 0), pipeline_mode=pl.Buffered(1)),
            pl.BlockSpec((tm, d), lambda i: (i, 0)),
            pl.BlockSpec((1, d), lambda i: (0, 0)),
        ],
        out_specs=pl.BlockSpec((tm, d), lambda i: (i, 0)),
        compiler_params=_cparams(("parallel",)),
        name=name,
    )(a, w, x, nw.reshape(1, d))


def _outproj_cast_kernel(a_ref, w_ref, x_ref, nw_ref, o_ref, wb_ref, acc_ref):
    kk = pl.program_id(0)

    @pl.when(kk == 0)
    def _():
        acc_ref[...] = jnp.zeros_like(acc_ref)

    w = w_ref[...].astype(BF16)
    wb_ref[...] = w
    acc_ref[...] += _dot(a_ref[...].astype(BF16), w)

    @pl.when(kk == pl.num_programs(0) - 1)
    def _():
        z = acc_ref[...]
        ms = jnp.mean(z * z, axis=-1, keepdims=True)
        o_ref[...] = x_ref[...] + z * lax.rsqrt(ms + NORM_EPS) * nw_ref[...]


def _outproj_norm_residual_cast(a, w, x, nw, *, tk, name):
    m, kd = a.shape
    d = w.shape[1]
    return pl.pallas_call(
        _outproj_cast_kernel,
        out_shape=[jax.ShapeDtypeStruct((m, d), F32), jax.ShapeDtypeStruct((kd, d), BF16)],
        grid=(kd // tk,),
        in_specs=[
            pl.BlockSpec((m, tk), lambda k: (0, k)),
            pl.BlockSpec((tk, d), lambda k: (k, 0)),
            pl.BlockSpec((m, d), lambda k: (0, 0)),
            pl.BlockSpec((1, d), lambda k: (0, 0)),
        ],
        out_specs=[pl.BlockSpec((m, d), lambda k: (0, 0)), pl.BlockSpec((tk, d), lambda k: (k, 0))],
        scratch_shapes=[pltpu.VMEM((m, d), F32)],
        compiler_params=_cparams(("arbitrary",)),
        name=name,
    )(a, w, x, nw.reshape(1, d))


def _diff_lambda(lq1_ref, lk1_ref, lq2_ref, lk2_ref):
    a = jnp.sum(lq1_ref[...] * lk1_ref[...], axis=-1, keepdims=True)
    b = jnp.sum(lq2_ref[...] * lk2_ref[...], axis=-1, keepdims=True)
    return jnp.exp(a) - jnp.exp(b) + LAMBDA_INIT


def _subln_gate(o, g, subln):
    ms = jnp.mean(o * o, axis=-1, keepdims=True)
    of = o * lax.rsqrt(ms + NORM_EPS) * subln * (1.0 - LAMBDA_INIT)
    return of * _silu(g)


def _attn_prompt_kernel(slopes_ref, q_ref, k_ref, v_ref, g_ref, subln_ref,
                        lq1_ref, lk1_ref, lq2_ref, lk2_ref, o_ref, m_ref, l_ref, acc_ref,
                        *, tq, tk, tsub):
    h = pl.program_id(1)
    qi = pl.program_id(2)
    slope2 = slopes_ref[h] * LOG2E
    scale2 = HALF ** -0.5 * LOG2E
    q0 = qi * tq
    m_ref[...] = jnp.full(m_ref.shape, -jnp.inf, F32)
    l_ref[...] = jnp.zeros(l_ref.shape, F32)
    acc_ref[...] = jnp.zeros(acc_ref.shape, F32)

    def kv_step(j, carry, masked):
        k0 = pl.multiple_of(j * tk, tk)
        vblk = v_ref[0, pl.ds(k0, tk), :]
        col = lax.broadcasted_iota(jnp.int32, (1, tk), 1) + (k0 - q0)
        bias = slope2 * col.astype(F32)
        for c in range(2):
            kc = k_ref[0, pl.ds(k0, tk), c * HALF:(c + 1) * HALF]
            for r0 in range(0, tq, tsub):
                rows = pl.ds(r0, tsub)
                s = _dot_nt(q_ref[0, rows, c * HALF:(c + 1) * HALF], kc) * scale2 + bias
                if masked:
                    row = lax.broadcasted_iota(jnp.int32, (tsub, 1), 0) + r0
                    s = jnp.where(row >= col, s, -jnp.inf)
                m_old = m_ref[c, rows, :]
                m_new = jnp.maximum(m_old, jnp.max(s, axis=-1, keepdims=True))
                alpha = jnp.exp2(m_old - m_new)
                p = jnp.exp2(s - _lane_tile(m_new, tk))
                m_ref[c, rows, :] = m_new
                l_ref[c, rows, :] = alpha * l_ref[c, rows, :] + jnp.sum(p, axis=-1, keepdims=True)
                acc_ref[c, rows, :] = (_lane_tile(alpha, HEAD_W) * acc_ref[c, rows, :]
                                       + _dot(p.astype(BF16), vblk))
        return carry

    n_full = qi * (tq // tk)
    lax.fori_loop(0, n_full, functools.partial(kv_step, masked=False), 0)
    for jj in range(tq // tk):
        kv_step(n_full + jj, 0, masked=True)

    lam = _diff_lambda(lq1_ref, lk1_ref, lq2_ref, lk2_ref)
    o = (acc_ref[0] / _lane_tile(l_ref[0], HEAD_W)
         - lam * (acc_ref[1] / _lane_tile(l_ref[1], HEAD_W)))
    o_ref[0] = _subln_gate(o, g_ref[0], subln_ref[...]).astype(o_ref.dtype)


def _attn_prompt(slopes, q, kb, vb, g, subln, lams, *, tq, tk, tsub):
    b, s, w = q.shape
    nh = w // HEAD_W
    blk_q = pl.BlockSpec((1, tq, HEAD_W), lambda bi, hi, qi: (bi, qi, hi))
    blk_kv = pl.BlockSpec((1, s, HEAD_W), lambda bi, hi, qi: (bi, 0, hi))
    vec = lambda n: pl.BlockSpec((1, n), lambda bi, hi, qi: (0, 0))
    kern = functools.partial(_attn_prompt_kernel, tq=tq, tk=tk, tsub=tsub)
    return pl.pallas_call(
        kern,
        out_shape=jax.ShapeDtypeStruct((b, s, w), BF16),
        grid=(b, nh, s // tq),
        in_specs=[_SMEM, blk_q, blk_kv, blk_kv, blk_q, vec(HEAD_W)] + [vec(HALF)] * 4,
        out_specs=blk_q,
        scratch_shapes=[pltpu.VMEM((2, tq, HALF), F32), pltpu.VMEM((2, tq, HALF), F32),
                        pltpu.VMEM((2, tq, HEAD_W), F32)],
        compiler_params=_cparams(("parallel", "parallel", "arbitrary")),
        name="diff_attn_prompt",
    )(slopes, q, kb, vb, g, subln.reshape(1, HEAD_W), *[v.reshape(1, HALF) for v in lams])


def _attn_sample_kernel(pt_ref, slopes_ref, q_ref, knew_ref, vnew_ref, g_ref, subln_ref,
                        lq1_ref, lk1_ref, lq2_ref, lk2_ref, *rest,
                        pages_per_step, n_steps, nh, t, page, past):
    del pt_ref
    pp = pages_per_step
    k_sets = (rest[:pp], rest[pp:2 * pp])
    v_sets = (rest[2 * pp:3 * pp], rest[3 * pp:4 * pp])
    o_ref = rest[4 * pp]
    wq_ref, bias_ref, sl_ref, m_ref, l_ref, acc_ref = rest[4 * pp + 1:]
    step = pl.program_id(1)
    scale = HALF ** -0.5
    nrow = nh * 2 * t
    lanes = page * nh
    log2 = lambda n: n.bit_length() - 1

    def row_head_tok():
        r = lax.broadcasted_iota(jnp.int32, (nrow, 1), 0)
        return jnp.right_shift(r, log2(2 * t)), jnp.bitwise_and(r, t - 1)

    def col_key_head(n):
        ln = lax.broadcasted_iota(jnp.int32, (1, n), 1)
        return jnp.right_shift(ln, log2(nh)), jnp.bitwise_and(ln, nh - 1)

    def slope_rows():
        rh, _ = row_head_tok()
        sl = jnp.zeros((nrow, 1), F32)
        for h in range(nh):
            sl = jnp.where(rh == h, slopes_ref[h], sl)
        return sl

    def alibi(dist_key0, n, causal):
        rh, rt = row_head_tok()
        key, ch = col_key_head(n)
        dist = dist_key0 + rt - key
        ok = ch == rh
        if causal:
            ok = ok & (dist >= 0)
        return jnp.where(ok, -slope_rows() * dist.astype(F32), -jnp.inf)

    @pl.when(step == 0)
    def _():
        q = q_ref[...]
        z = jnp.zeros((t, HALF), F32)
        for h in range(nh):
            q1 = q[:, h * HEAD_W: h * HEAD_W + HALF]
            q2 = q[:, h * HEAD_W + HALF: (h + 1) * HEAD_W]
            top = jnp.concatenate([q1, z], axis=-1)
            bot = jnp.concatenate([z, q2], axis=-1)
            wq_ref[h * 2 * t:(h + 1) * 2 * t, :] = jnp.concatenate([top, bot], axis=0).astype(BF16)
        bias_ref[...] = alibi(past, lanes, causal=False)
        sl_ref[...] = jnp.broadcast_to(slope_rows(), sl_ref.shape)
        m_ref[...] = jnp.full(m_ref.shape, -jnp.inf, F32)
        l_ref[...] = jnp.zeros(l_ref.shape, F32)
        acc_ref[...] = jnp.zeros(acc_ref.shape, F32)

    def update(k2, v2, bias, row_shift):
        st = _dot_nt(wq_ref[...], k2.astype(BF16)) * scale + bias
        m_old = m_ref[...]
        m_new = jnp.maximum(m_old, jnp.max(st, axis=-1, keepdims=True) + row_shift)
        alpha = jnp.exp(m_old - m_new)
        p = jnp.exp(st + _lane_tile(row_shift - m_new, st.shape[1]))
        m_ref[...] = m_new
        l_ref[...] = alpha * l_ref[...] + jnp.sum(p, axis=-1, keepdims=True)
        acc_ref[...] = (_lane_tile(alpha, HEAD_W) * acc_ref[...]
                        + _dot(p.astype(BF16), v2.astype(BF16)))

    for parity in range(2):
        @pl.when(step % 2 == parity)
        def _(parity=parity):
            for r in range(pp):
                pg = step * pp + r
                update(k_sets[parity][r][...], v_sets[parity][r][...], bias_ref[...],
                       sl_ref[...] * (pg * page).astype(F32))

    @pl.when(step == n_steps - 1)
    def _():
        n_new = HALF
        update(_pad_rows(knew_ref[...], n_new), _pad_rows(vnew_ref[...], n_new),
               alibi(0, n_new, causal=True), jnp.zeros(sl_ref.shape, F32))
        lam = _diff_lambda(lq1_ref, lk1_ref, lq2_ref, lk2_ref)
        on = acc_ref[...] / _lane_tile(l_ref[...], HEAD_W)
        g = g_ref[...]
        for h in range(nh):
            o = on[h * 2 * t: h * 2 * t + t] - lam * on[h * 2 * t + t: (h + 1) * 2 * t]
            o_ref[:, h * HEAD_W:(h + 1) * HEAD_W] = _subln_gate(
                o, g[:, h * HEAD_W:(h + 1) * HEAD_W], subln_ref[...])


def _attn_sample(page_table, slopes, q, k_new, v_new, g, cache_k, cache_v, subln, lams,
                 *, pages_per_step):
    db, t, w = q.shape
    nh = w // HEAD_W
    n_phys, page = cache_k.shape[:2]
    n_pages = page_table.shape[1]
    past = n_pages * page
    pp = pages_per_step
    n_steps = n_pages // pp
    nrow = nh * 2 * t
    assert n_steps * pp == n_pages and nrow % BF16_ROWS == 0
    assert nh & (nh - 1) == 0 and t & (t - 1) == 0 and t * nh <= HALF
    ck = cache_k.reshape(n_phys, page * nh, HEAD_W)
    cv = cache_v.reshape(n_phys, page * nh, HEAD_W)
    kn = k_new.reshape(db, t * nh, HEAD_W)
    vn = v_new.reshape(db, t * nh, HEAD_W)

    assert n_steps % 2 == 0
    n_groups = db * n_steps

    def _page_maps(parity, r):
        def index_map(b, s, pt):
            g = b * n_steps + s
            if parity == 0:
                g = jnp.minimum(g + (g % 2), n_groups - 2)
            else:
                g = g - (g % 2) + 1
            return (pt[g // n_steps, (g % n_steps) * pp + r], 0, 0)
        return index_map

    per_b = lambda shape: pl.BlockSpec((None,) + shape, lambda b, s, pt: (b, 0, 0))
    vec = lambda n: pl.BlockSpec((1, n), lambda b, s, pt: (0, 0))
    page_blk = (None, page * nh, HEAD_W)
    page_specs = [pl.BlockSpec(page_blk, _page_maps(parity, r))
                  for parity in range(2) for r in range(pp)]
    in_specs = ([_SMEM, per_b((t, w)), per_b((t * nh, HEAD_W)), per_b((t * nh, HEAD_W)),
                 per_b((t, w)), vec(HEAD_W)] + [vec(HALF)] * 4 + page_specs * 2)
    kern = functools.partial(_attn_sample_kernel, pages_per_step=pp, n_steps=n_steps, nh=nh, t=t,
                             page=page, past=past)
    return pl.pallas_call(
        kern,
        out_shape=jax.ShapeDtypeStruct((db, t, w), F32),
        grid_spec=pltpu.PrefetchScalarGridSpec(
            num_scalar_prefetch=1,
            grid=(db, n_steps),
            in_specs=in_specs,
            out_specs=per_b((t, w)),
            scratch_shapes=[
                pltpu.VMEM((nrow, HEAD_W), BF16),
                pltpu.VMEM((nrow, page * nh), F32),
                pltpu.VMEM((nrow, HALF), F32),
                pltpu.VMEM((nrow, HALF), F32),
                pltpu.VMEM((nrow, HALF), F32),
                pltpu.VMEM((nrow, HEAD_W), F32),
            ],
        ),
        compiler_params=_cparams(("arbitrary", "arbitrary")),
        name="diff_attn_sample",
    )(page_table, slopes, q, kn, vn, g, subln.reshape(1, HEAD_W),
      *[v.reshape(1, HALF) for v in lams], *([ck] * (2 * pp)), *([cv] * (2 * pp)))


def _retention_kernel(logg_ref, q_ref, k_ref, v_ref, g_ref, *rest, chunk, n_chunks, has_init):
    if has_init:
        s0_ref, o_ref, s_ref, decay_ref = rest
    else:
        s0_ref = None
        o_ref, s_ref, decay_ref = rest
    h = pl.program_id(1)
    lg = jnp.full((1, 1), logg_ref[h], F32)
    c = max(chunk, BF16_ROWS)
    ri = lax.broadcasted_iota(jnp.int32, (c, 1), 0)
    rif = ri.astype(F32)

    @pl.when(pl.program_id(2) == 0)
    def _():
        if has_init:
            s_ref[0, 0] = s0_ref[0, 0]
        else:
            s_ref[0, 0] = jnp.zeros(s_ref.shape[2:], F32)
        diff = ri - lax.broadcasted_iota(jnp.int32, (1, c), 1)
        decay_ref[...] = jnp.where(diff >= 0, jnp.exp(lg * jnp.maximum(diff, 0).astype(F32)), 0.0)

    d_query = jnp.exp(lg * (rif + 1.0))
    d_key = jnp.exp(lg * (chunk - 1.0 - rif))
    d_chunk = jnp.exp(lg * float(chunk))
    state = s_ref[0, 0]
    for n in range(n_chunks):
        rows = pl.ds(n * chunk, chunk)
        q = _pad_rows(q_ref[0, rows, :], c).astype(BF16)
        kf = _pad_rows(k_ref[0, rows, :], c).astype(F32)
        v = _pad_rows(v_ref[0, rows, :], c).astype(BF16)
        scores = _dot_nt(q, kf.astype(BF16)) * decay_ref[...]
        o = _dot(scores.astype(BF16), v) + _dot(q, state.astype(BF16)) * d_query
        state = d_chunk * state + _dot_tn((kf * d_key).astype(BF16), v)
        o = o[:chunk]
        ms = jnp.mean(o * o, axis=-1, keepdims=True)
        o_ref[0, rows, :] = (o * lax.rsqrt(ms + NORM_EPS) * _silu(g_ref[0, rows, :])).astype(o_ref.dtype)
    s_ref[0, 0] = state


def _retention(logg, q, k, v, g, s0, *, chunk, chunks_per_step, out_dtype):
    b, s, wqk = q.shape
    nh = logg.shape[0]
    dk = wqk // nh
    dv = v.shape[2] // nh
    rows = chunk * chunks_per_step
    blk = lambda wd: pl.BlockSpec((1, rows, wd), lambda bi, hi, ci: (bi, ci, hi))
    st_blk = pl.BlockSpec((1, 1, dk, dv), lambda bi, hi, ci: (bi, hi, 0, 0))
    in_specs = [_SMEM, blk(dk), blk(dk), blk(dv), blk(dv)]
    args = [logg, q, k, v, g]
    if s0 is not None:
        in_specs.append(st_blk)
        args.append(s0)
    kern = functools.partial(_retention_kernel, chunk=chunk, n_chunks=chunks_per_step,
                             has_init=s0 is not None)
    c = max(chunk, BF16_ROWS)
    return pl.pallas_call(
        kern,
        out_shape=[jax.ShapeDtypeStruct((b, s, nh * dv), out_dtype),
                   jax.ShapeDtypeStruct((b, nh, dk, dv), F32)],
        grid=(b, nh, s // rows),
        in_specs=in_specs,
        out_specs=[blk(dv), st_blk],
        scratch_shapes=[pltpu.VMEM((c, c), F32)],
        compiler_params=_cparams(("parallel", "parallel", "arbitrary")),
        name="retention",
    )(*args)


def _rope_tables(pos, half):
    inv_freq = 1.0 / (ROPE_BASE ** jnp.linspace(0.0, 1.0, half, dtype=F32))
    ang = pos.astype(F32)[:, None] * inv_freq[None, :]
    return jnp.cos(ang), jnp.sin(ang)


def kernel(x_prompt, x_sample, cache_k, cache_v, page_table, state_ret, norm_pre_attn, w_in_attn, lambda_q1, lambda_k1, lambda_q2, lambda_k2, subln_w_attn, w_out_attn, norm_post_attn, norm_pre_ret, w_in_ret, w_out_ret, norm_post_ret):
    b, s, d = x_prompt.shape
    db, t, _ = x_sample.shape
    nh_a = cache_k.shape[2]
    nh_r = state_ret.shape[1]
    dk_r, dv_r = state_ret.shape[2:]
    past = page_table.shape[1] * cache_k.shape[1]
    wa = nh_a * HEAD_W
    wqk_r, wv_r = nh_r * dk_r, nh_r * dv_r

    slopes = jnp.exp2(-8.0 * jnp.arange(1, nh_a + 1, dtype=F32) / nh_a)
    logg = jnp.log1p(-jnp.exp2(-5.0 - jnp.arange(nh_r, dtype=F32)))
    lams = (lambda_q1, lambda_k1, lambda_q2, lambda_k2)
    xp = x_prompt.reshape(b * s, d)
    xs = x_sample.reshape(db * t, d)

    segs_s = [(wa, [(F32, "q")]), (wa, [(F32, "k")]), (wa, [(F32, "v")]), (wa, [(F32, "g")])]
    qs, ks, vs, gs, *w_in_a = _norm_inproj(xs, norm_pre_attn, w_in_attn, segs_s, tm=1024,
                                           n_col_tiles=8, name="inproj_attn_sample")
    segs_p = [(wa, [(BF16, "q")]), (wa, [(F32, "k"), (BF16, "kb")]),
              (wa, [(F32, "v"), (BF16, "vb")]), (wa, [(F32, "g")])]
    qp, kp, kbp, vp, vbp, gp = _norm_inproj(xp, norm_pre_attn, w_in_a, segs_p, tm=1024, n_col_tiles=8,
                                            name="inproj_attn_prompt")

    r3 = lambda a, n: a.reshape(n, -1, a.shape[-1])
    gated_p = _attn_prompt(slopes, r3(qp, b), r3(kbp, b), r3(vbp, b), r3(gp, b), subln_w_attn, lams,
                           tq=512, tk=512, tsub=128)
    gated_s = _attn_sample(page_table, slopes, r3(qs, db), ks.reshape(db, t, nh_a, HEAD_W),
                           vs.reshape(db, t, nh_a, HEAD_W), r3(gs, db), cache_k, cache_v,
                           subln_w_attn, lams, pages_per_step=4)

    y1s, w_out_a = _outproj_norm_residual_cast(gated_s.reshape(db * t, wa), w_out_attn, xs,
                                               norm_post_attn, tk=512, name="outproj_attn_sample")
    y1p = _outproj_norm_residual(gated_p.reshape(b * s, wa), w_out_a, xp, norm_post_attn,
                                 tm=512, tsub=128, name="outproj_attn_prompt")

    cos_p, sin_p = _rope_tables(jnp.arange(s, dtype=jnp.int32), dk_r // 2)
    cos_s, sin_s = _rope_tables(past + jnp.arange(t, dtype=jnp.int32), dk_r // 2)
    cos_s, sin_s = jnp.tile(cos_s, (db, 1)), jnp.tile(sin_s, (db, 1))
    rs = (1.0, dk_r ** -0.5)
    segs_rs = [(wqk_r, [(F32, "rot_q")]), (wqk_r, [(F32, "rot_k")]),
               (wv_r, [(F32, "v")]), (wv_r, [(F32, "g")])]
    sq, sk, sv, sg, *w_in_r = _norm_inproj(y1s, norm_pre_ret, w_in_ret, segs_rs, tm=1024, n_col_tiles=8,
                                           cos=cos_s, sin=sin_s, rotate_scale=rs,
                                           name="inproj_ret_sample")
    segs_rp = [(wqk_r, [(BF16, "rot_q")]), (wqk_r, [(BF16, "rot_k")]),
               (wv_r, [(BF16, "v")]), (wv_r, [(F32, "g")])]
    rq, rk, rv, rg = _norm_inproj(y1p, norm_pre_ret, w_in_r, segs_rp, tm=1024, n_col_tiles=8,
                                  cos=cos_p, sin=sin_p, rotate_scale=rs, name="inproj_ret_prompt")

    gated_rp, state_p = _retention(logg, r3(rq, b), r3(rk, b), r3(rv, b), r3(rg, b), None,
                                   chunk=256, chunks_per_step=4, out_dtype=BF16)
    gated_rs, state_s = _retention(logg, r3(sq, db), r3(sk, db), r3(sv, db), r3(sg, db),
                                   state_ret.astype(F32), chunk=t, chunks_per_step=1, out_dtype=F32)

    y2s, w_out_r = _outproj_norm_residual_cast(gated_rs.reshape(db * t, wv_r), w_out_ret, y1s,
                                               norm_post_ret, tk=512, name="outproj_ret_sample")
    y2p = _outproj_norm_residual(gated_rp.reshape(b * s, wv_r), w_out_r, y1p, norm_post_ret,
                                 tm=512, tsub=128, name="outproj_ret_prompt")

    return (y2p.reshape(b, s, d), y2s.reshape(db, t, d),
            kp.reshape(b, s, nh_a, HEAD_W), vp.reshape(b, s, nh_a, HEAD_W), state_p,
            ks.reshape(db, t, nh_a, HEAD_W), vs.reshape(db, t, nh_a, HEAD_W), state_s)
```

```python
import functools
import math

import jax
import jax.numpy as jnp
from jax import lax
from jax.experimental import pallas as pl
from jax.experimental.pallas import tpu as pltpu

F32 = jnp.float32
BF16 = jnp.bfloat16

NORM_EPS = 1e-6
LAMBDA_INIT = 0.8 - 0.6 * math.exp(-0.3 * 0)
ROPE_BASE = 10000.0
LOG2E = math.log2(math.e)
HEAD_W = 256
HALF = 128
ATTN_Q_SCALE = HALF ** -0.5 * LOG2E
BF16_ROWS = 16
VMEM_LIMIT_BYTES = 56 * 1024 * 1024

_SMEM = pl.BlockSpec(memory_space=pltpu.SMEM)


def _cparams(sem):
    return pltpu.CompilerParams(dimension_semantics=sem, vmem_limit_bytes=VMEM_LIMIT_BYTES)


def _silu(g):
    return g * (1.0 / (1.0 + jnp.exp(-g)))


def _dot_nt(a, b):
    return lax.dot_general(a, b, (((1,), (1,)), ((), ())), preferred_element_type=F32)


def _dot_tn(a, b):
    return lax.dot_general(a, b, (((0,), (0,)), ((), ())), preferred_element_type=F32)


def _dot(a, b):
    return jnp.dot(a, b, preferred_element_type=F32)


def _pad_rows(x, n):
    if x.shape[0] == n:
        return x
    return jnp.concatenate([x, jnp.zeros((n - x.shape[0], x.shape[1]), x.dtype)], axis=0)


def _lane_tile(x, width):
    return jnp.concatenate([x] * (width // x.shape[1]), axis=-1)


def _inproj_kernel(*refs, seg_kinds, rotate_scale, emit_bf16_weights):
    nseg = len(seg_kinds)
    x_ref, nw_ref = refs[:2]
    w_refs = refs[2:2 + nseg]
    n_in = 2 + nseg
    cos_ref = sin_ref = None
    if rotate_scale is not None:
        cos_ref, sin_ref = refs[n_in:n_in + 2]
        n_in += 2
    h_ref = refs[-1]
    out_refs = refs[n_in:-1]
    if emit_bf16_weights:
        out_refs, wb_refs = out_refs[:-nseg], out_refs[-nseg:]
    else:
        wb_refs = (None,) * nseg

    @pl.when(pl.program_id(1) == 0)
    def _():
        x = x_ref[...]
        ms = jnp.mean(x * x, axis=-1, keepdims=True)
        h_ref[...] = (x * lax.rsqrt(ms + NORM_EPS) * nw_ref[...]).astype(BF16)

    def rotated(a, scale):
        cos = cos_ref[...]
        sin = sin_ref[...]
        parts = []
        for hh in range(a.shape[1] // HEAD_W):
            x1 = a[:, hh * HEAD_W: hh * HEAD_W + HALF]
            x2 = a[:, hh * HEAD_W + HALF: (hh + 1) * HEAD_W]
            parts.append((x1 * cos - x2 * sin) * scale)
            parts.append((x1 * sin + x2 * cos) * scale)
        return jnp.concatenate(parts, axis=-1)

    oi = 0
    for w_ref, wb_ref, kinds in zip(w_refs, wb_refs, seg_kinds):
        w = w_ref[...]
        if wb_ref is not None:
            w = w.astype(BF16)
            wb_ref[...] = w
        acc = _dot(h_ref[...], w)
        for kind in kinds:
            ref = out_refs[oi]
            oi += 1
            if kind == "rot_q":
                ref[...] = rotated(acc, rotate_scale[0]).astype(ref.dtype)
            elif kind == "rot_k":
                ref[...] = rotated(acc, rotate_scale[1]).astype(ref.dtype)
            elif kind == "q_log2":
                ref[...] = (acc * ATTN_Q_SCALE).astype(ref.dtype)
            else:
                ref[...] = acc.astype(ref.dtype)


def _norm_inproj(x, nw, w, segs, *, tm, n_col_tiles, cos=None, sin=None, rotate_scale=None, name):
    m, d = x.shape
    tm = min(tm, m)
    nj = n_col_tiles
    emit = not isinstance(w, (list, tuple))
    assert not emit or m == tm
    seg_kinds = tuple(tuple(kind for _, kind in outs) for _, outs in segs)
    in_specs = [
        pl.BlockSpec((tm, d), lambda i, j: (i, 0)),
        pl.BlockSpec((1, d), lambda i, j: (0, 0)),
    ]
    args = [x, nw.reshape(1, d)]
    out_shapes = []
    out_specs = []
    wb_shapes = []
    wb_specs = []
    off = 0
    for si, (width, outs) in enumerate(segs):
        tw = width // nj
        assert tw * nj == width and off % tw == 0 and tw % HEAD_W == 0
        if emit:
            in_specs.append(pl.BlockSpec((d, tw), lambda i, j, o=off // tw: (0, o + j)))
            args.append(w)
            wb_shapes.append(jax.ShapeDtypeStruct((d, width), BF16))
            wb_specs.append(pl.BlockSpec((d, tw), lambda i, j: (0, j)))
        else:
            assert w[si].shape == (d, width)
            in_specs.append(pl.BlockSpec((d, tw), lambda i, j: (0, j)))
            args.append(w[si])
        for dtype, _ in outs:
            out_shapes.append(jax.ShapeDtypeStruct((m, width), dtype))
            out_specs.append(pl.BlockSpec((tm, tw), lambda i, j: (i, j)))
        off += width
    assert not emit or off == w.shape[1]
    if rotate_scale is not None:
        nblk = cos.shape[0] // tm
        in_specs += [pl.BlockSpec((tm, HALF), lambda i, j: (i % nblk, 0))] * 2
        args += [cos, sin]
    out_shapes += wb_shapes
    out_specs += wb_specs
    kern = functools.partial(_inproj_kernel, seg_kinds=seg_kinds, rotate_scale=rotate_scale,
                             emit_bf16_weights=emit)
    return pl.pallas_call(
        kern,
        out_shape=out_shapes,
        grid=(m // tm, nj),
        in_specs=in_specs,
        out_specs=out_specs,
        scratch_shapes=[pltpu.VMEM((tm, d), BF16)],
        compiler_params=_cparams(("parallel", "arbitrary")),
        name=name,
    )(*args)


def _outproj_kernel(a_ref, w_ref, x_ref, nw_ref, o_ref, *, tsub):
    for r0 in range(0, a_ref.shape[0], tsub):
        rows = pl.ds(r0, tsub)
        z = _dot(a_ref[rows, :].astype(BF16), w_ref[...])
        ms = jnp.mean(z * z, axis=-1, keepdims=True)
        o_ref[rows, :] = x_ref[rows, :] + z * lax.rsqrt(ms + NORM_EPS) * nw_ref[...]


def _outproj_norm_residual(a, w, x, nw, *, tm, tsub, name):
    m, kd = a.shape
    d = w.shape[1]
    tm = min(tm, m)
    tsub = min(tsub, tm)
    return pl.pallas_call(
        functools.partial(_outproj_kernel, tsub=tsub),
        out_shape=jax.ShapeDtypeStruct((m, d), F32),
        grid=(m // tm,),
        in_specs=[
            pl.BlockSpec((tm, kd), lambda i: (i, 0)),
            pl.BlockSpec((kd, d), lambda i: (0, 0), pipeline_mode=pl.Buffered(1)),
            pl.BlockSpec((tm, d), lambda i: (i, 0)),
            pl.BlockSpec((1, d), lambda i: (0, 0)),
        ],
        out_specs=pl.BlockSpec((tm, d), lambda i: (i, 0)),
        compiler_params=_cparams(("parallel",)),
        name=name,
    )(a, w, x, nw.reshape(1, d))


def _outproj_cast_kernel(a_ref, w_ref, x_ref, nw_ref, o_ref, wb_ref, acc_ref):
    kk = pl.program_id(0)

    @pl.when(kk == 0)
    def _():
        acc_ref[...] = jnp.zeros_like(acc_ref)

    w = w_ref[...].astype(BF16)
    wb_ref[...] = w
    acc_ref[...] += _dot(a_ref[...].astype(BF16), w)

    @pl.when(kk == pl.num_programs(0) - 1)
    def _():
        z = acc_ref[...]
        ms = jnp.mean(z * z, axis=-1, keepdims=True)
        o_ref[...] = x_ref[...] + z * lax.rsqrt(ms + NORM_EPS) * nw_ref[...]


def _outproj_norm_residual_cast(a, w, x, nw, *, tk, name):
    m, kd = a.shape
    d = w.shape[1]
    return pl.pallas_call(
        _outproj_cast_kernel,
        out_shape=[jax.ShapeDtypeStruct((m, d), F32), jax.ShapeDtypeStruct((kd, d), BF16)],
        grid=(kd // tk,),
        in_specs=[
            pl.BlockSpec((m, tk), lambda k: (0, k)),
            pl.BlockSpec((tk, d), lambda k: (k, 0)),
            pl.BlockSpec((m, d), lambda k: (0, 0)),
            pl.BlockSpec((1, d), lambda k: (0, 0)),
        ],
        out_specs=[pl.BlockSpec((m, d), lambda k: (0, 0)), pl.BlockSpec((tk, d), lambda k: (k, 0))],
        scratch_shapes=[pltpu.VMEM((m, d), F32)],
        compiler_params=_cparams(("arbitrary",)),
        name=name,
    )(a, w, x, nw.reshape(1, d))


def _diff_lambda(lq1_ref, lk1_ref, lq2_ref, lk2_ref):
    a = jnp.sum(lq1_ref[...] * lk1_ref[...], axis=-1, keepdims=True)
    b = jnp.sum(lq2_ref[...] * lk2_ref[...], axis=-1, keepdims=True)
    return jnp.exp(a) - jnp.exp(b) + LAMBDA_INIT


def _subln_gate(o, g, subln):
    ms = jnp.mean(o * o, axis=-1, keepdims=True)
    of = o * lax.rsqrt(ms + NORM_EPS) * subln * (1.0 - LAMBDA_INIT)
    return of * _silu(g)


def _attn_prompt_kernel(slopes_ref, q_ref, k_ref, v_ref, g_ref, subln_ref,
                        lq1_ref, lk1_ref, lq2_ref, lk2_ref, o_ref, m_ref, l_ref, acc_ref,
                        *, tq, tk, tsub):
    h = pl.program_id(1)
    qi = pl.program_id(2)
    slope2 = slopes_ref[h] * LOG2E
    q0 = qi * tq
    m_ref[...] = jnp.full(m_ref.shape, -jnp.inf, F32)
    l_ref[...] = jnp.zeros(l_ref.shape, F32)
    acc_ref[...] = jnp.zeros(acc_ref.shape, F32)

    def kv_step(j, carry, diagonal):
        k0 = pl.multiple_of(j * tk, tk)
        col = lax.broadcasted_iota(jnp.int32, (1, tk), 1) + (k0 - q0)
        bias = slope2 * col.astype(F32)
        for c in range(2):
            for r0 in range(0, tq, tsub):
                rows = pl.ds(r0, tsub)
                nk = min(r0 + tsub, tk) if diagonal else tk
                kc = k_ref[0, pl.ds(k0, nk), c * HALF:(c + 1) * HALF]
                s = _dot_nt(q_ref[0, rows, c * HALF:(c + 1) * HALF], kc) + bias[:, :nk]
                if diagonal:
                    row = lax.broadcasted_iota(jnp.int32, (tsub, 1), 0) + r0
                    s = jnp.where(row >= col[:, :nk], s, -jnp.inf)
                m_old = m_ref[c, rows, :]
                m_new = jnp.maximum(m_old, jnp.max(s, axis=-1, keepdims=True))
                alpha = jnp.exp2(m_old - m_new)
                p = jnp.exp2(s - _lane_tile(m_new, nk))
                m_ref[c, rows, :] = m_new
                l_ref[c, rows, :] = alpha * l_ref[c, rows, :] + jnp.sum(p, axis=-1, keepdims=True)
                acc_ref[c, rows, :] = (_lane_tile(alpha, HEAD_W) * acc_ref[c, rows, :]
                                       + _dot(p.astype(BF16), v_ref[0, pl.ds(k0, nk), :]))
        return carry

    lax.fori_loop(0, qi, functools.partial(kv_step, diagonal=False), 0)
    kv_step(qi, 0, diagonal=True)

    lam = _diff_lambda(lq1_ref, lk1_ref, lq2_ref, lk2_ref)
    o = (acc_ref[0] / _lane_tile(l_ref[0], HEAD_W)
         - lam * (acc_ref[1] / _lane_tile(l_ref[1], HEAD_W)))
    o_ref[0] = _subln_gate(o, g_ref[0], subln_ref[...]).astype(o_ref.dtype)


def _attn_prompt(slopes, q, kb, vb, g, subln, lams, *, tq, tk, tsub):
    b, s, w = q.shape
    nh = w // HEAD_W
    blk_q = pl.BlockSpec((1, tq, HEAD_W), lambda bi, hi, qi: (bi, qi, hi))
    blk_kv = pl.BlockSpec((1, s, HEAD_W), lambda bi, hi, qi: (bi, 0, hi))
    vec = lambda n: pl.BlockSpec((1, n), lambda bi, hi, qi: (0, 0))
    assert tq == tk and tq % tsub == 0
    kern = functools.partial(_attn_prompt_kernel, tq=tq, tk=tk, tsub=tsub)
    return pl.pallas_call(
        kern,
        out_shape=jax.ShapeDtypeStruct((b, s, w), BF16),
        grid=(b, nh, s // tq),
        in_specs=[_SMEM, blk_q, blk_kv, blk_kv, blk_q, vec(HEAD_W)] + [vec(HALF)] * 4,
        out_specs=blk_q,
        scratch_shapes=[pltpu.VMEM((2, tq, HALF), F32), pltpu.VMEM((2, tq, HALF), F32),
                        pltpu.VMEM((2, tq, HEAD_W), F32)],
        compiler_params=_cparams(("parallel", "parallel", "arbitrary")),
        name="diff_attn_prompt",
    )(slopes, q, kb, vb, g, subln.reshape(1, HEAD_W), *[v.reshape(1, HALF) for v in lams])


def _attn_sample_kernel(pt_ref, slopes_ref, q_ref, knew_ref, vnew_ref, g_ref, subln_ref,
                        lq1_ref, lk1_ref, lq2_ref, lk2_ref, *rest,
                        pages_per_step, n_steps, nh, t, page, past):
    del pt_ref
    pp = pages_per_step
    k_refs = rest[:pp]
    v_refs = rest[pp:2 * pp]
    o_ref = rest[2 * pp]
    wq_ref, bias_ref, sl_ref, m_ref, l_ref, acc_ref = rest[2 * pp + 1:]
    step = pl.program_id(1)
    scale = HALF ** -0.5
    nrow = nh * 2 * t
    lanes = page * nh
    log2 = lambda n: n.bit_length() - 1

    def row_head_tok():
        r = lax.broadcasted_iota(jnp.int32, (nrow, 1), 0)
        return jnp.right_shift(r, log2(2 * t)), jnp.bitwise_and(r, t - 1)

    def col_key_head(n):
        ln = lax.broadcasted_iota(jnp.int32, (1, n), 1)
        return jnp.right_shift(ln, log2(nh)), jnp.bitwise_and(ln, nh - 1)

    def slope_rows():
        rh, _ = row_head_tok()
        sl = jnp.zeros((nrow, 1), F32)
        for h in range(nh):
            sl = jnp.where(rh == h, slopes_ref[h], sl)
        return sl

    def alibi(dist_key0, n, causal):
        rh, rt = row_head_tok()
        key, ch = col_key_head(n)
        dist = dist_key0 + rt - key
        ok = ch == rh
        if causal:
            ok = ok & (dist >= 0)
        return jnp.where(ok, -slope_rows() * dist.astype(F32), -jnp.inf)

    @pl.when(step == 0)
    def _():
        q = q_ref[...]
        z = jnp.zeros((t, HALF), F32)
        for h in range(nh):
            q1 = q[:, h * HEAD_W: h * HEAD_W + HALF]
            q2 = q[:, h * HEAD_W + HALF: (h + 1) * HEAD_W]
            top = jnp.concatenate([q1, z], axis=-1)
            bot = jnp.concatenate([z, q2], axis=-1)
            wq_ref[h * 2 * t:(h + 1) * 2 * t, :] = jnp.concatenate([top, bot], axis=0).astype(BF16)
        bias_ref[...] = alibi(past, lanes, causal=False)
        sl_ref[...] = jnp.broadcast_to(slope_rows(), sl_ref.shape)
        m_ref[...] = jnp.full(m_ref.shape, -jnp.inf, F32)
        l_ref[...] = jnp.zeros(l_ref.shape, F32)
        acc_ref[...] = jnp.zeros(acc_ref.shape, F32)

    def update(k2, v2, bias, row_shift):
        st = _dot_nt(wq_ref[...], k2.astype(BF16)) * scale + bias
        m_old = m_ref[...]
        m_new = jnp.maximum(m_old, jnp.max(st, axis=-1, keepdims=True) + row_shift)
        alpha = jnp.exp(m_old - m_new)
        p = jnp.exp(st + _lane_tile(row_shift - m_new, st.shape[1]))
        m_ref[...] = m_new
        l_ref[...] = alpha * l_ref[...] + jnp.sum(p, axis=-1, keepdims=True)
        acc_ref[...] = (_lane_tile(alpha, HEAD_W) * acc_ref[...]
                        + _dot(p.astype(BF16), v2.astype(BF16)))

    for r in range(pp):
        pg = step * pp + r
        update(k_refs[r][...], v_refs[r][...], bias_ref[...],
               sl_ref[...] * (pg * page).astype(F32))

    @pl.when(step == n_steps - 1)
    def _():
        n_new = HALF
        update(_pad_rows(knew_ref[...], n_new), _pad_rows(vnew_ref[...], n_new),
               alibi(0, n_new, causal=True), jnp.zeros(sl_ref.shape, F32))
        lam = _diff_lambda(lq1_ref, lk1_ref, lq2_ref, lk2_ref)
        on = acc_ref[...] / _lane_tile(l_ref[...], HEAD_W)
        g = g_ref[...]
        for h in range(nh):
            o = on[h * 2 * t: h * 2 * t + t] - lam * on[h * 2 * t + t: (h + 1) * 2 * t]
            o_ref[:, h * HEAD_W:(h + 1) * HEAD_W] = _subln_gate(
                o, g[:, h * HEAD_W:(h + 1) * HEAD_W], subln_ref[...])


def _attn_sample(page_table, slopes, q, k_new, v_new, g, cache_k, cache_v, subln, lams,
                 *, pages_per_step):
    db, t, w = q.shape
    nh = w // HEAD_W
    n_phys, page = cache_k.shape[:2]
    n_pages = page_table.shape[1]
    past = n_pages * page
    pp = pages_per_step
    n_steps = n_pages // pp
    nrow = nh * 2 * t
    assert n_steps * pp == n_pages and nrow % BF16_ROWS == 0
    assert nh & (nh - 1) == 0 and t & (t - 1) == 0 and t * nh <= HALF
    ck = cache_k.reshape(n_phys, page * nh, HEAD_W)
    cv = cache_v.reshape(n_phys, page * nh, HEAD_W)
    kn = k_new.reshape(db, t * nh, HEAD_W)
    vn = v_new.reshape(db, t * nh, HEAD_W)

    def page_map(r):
        return lambda b, s, pt: (pt[b, s * pp + r], 0, 0)

    per_b = lambda shape: pl.BlockSpec((None,) + shape, lambda b, s, pt: (b, 0, 0))
    vec = lambda n: pl.BlockSpec((1, n), lambda b, s, pt: (0, 0))
    page_blk = (None, page * nh, HEAD_W)
    page_specs = [pl.BlockSpec(page_blk, page_map(r)) for r in range(pp)]
    in_specs = ([_SMEM, per_b((t, w)), per_b((t * nh, HEAD_W)), per_b((t * nh, HEAD_W)),
                 per_b((t, w)), vec(HEAD_W)] + [vec(HALF)] * 4 + page_specs * 2)
    kern = functools.partial(_attn_sample_kernel, pages_per_step=pp, n_steps=n_steps, nh=nh, t=t,
                             page=page, past=past)
    return pl.pallas_call(
        kern,
        out_shape=jax.ShapeDtypeStruct((db, t, w), F32),
        grid_spec=pltpu.PrefetchScalarGridSpec(
            num_scalar_prefetch=1,
            grid=(db, n_steps),
            in_specs=in_specs,
            out_specs=per_b((t, w)),
            scratch_shapes=[
                pltpu.VMEM((nrow, HEAD_W), BF16),
                pltpu.VMEM((nrow, page * nh), F32),
                pltpu.VMEM((nrow, HALF), F32),
                pltpu.VMEM((nrow, HALF), F32),
                pltpu.VMEM((nrow, HALF), F32),
                pltpu.VMEM((nrow, HEAD_W), F32),
            ],
        ),
        compiler_params=_cparams(("parallel", "arbitrary")),
        name="diff_attn_sample",
    )(page_table, slopes, q, kn, vn, g, subln.reshape(1, HEAD_W),
      *[v.reshape(1, HALF) for v in lams], *([ck] * pp), *([cv] * pp))


def _retention_kernel(logg_ref, q_ref, k_ref, v_ref, g_ref, *rest, chunk, n_chunks, has_init):
    if has_init:
        s0_ref, o_ref, s_ref, decay_ref = rest
    else:
        s0_ref = None
        o_ref, s_ref, decay_ref = rest
    h = pl.program_id(1)
    lg = jnp.full((1, 1), logg_ref[h], F32)
    c = max(chunk, BF16_ROWS)
    ri = lax.broadcasted_iota(jnp.int32, (c, 1), 0)
    rif = ri.astype(F32)

    @pl.when(pl.program_id(2) == 0)
    def _():
        if has_init:
            s_ref[0, 0] = s0_ref[0, 0]
        else:
            s_ref[0, 0] = jnp.zeros(s_ref.shape[2:], F32)
        diff = ri - lax.broadcasted_iota(jnp.int32, (1, c), 1)
        decay_ref[...] = jnp.where(diff >= 0, jnp.exp(lg * jnp.maximum(diff, 0).astype(F32)), 0.0)

    d_query = jnp.exp(lg * (rif + 1.0))
    d_key = jnp.exp(lg * (chunk - 1.0 - rif))
    d_chunk = jnp.exp(lg * float(chunk))
    state = s_ref[0, 0]
    for n in range(n_chunks):
        rows = pl.ds(n * chunk, chunk)
        q = _pad_rows(q_ref[0, rows, :], c).astype(BF16)
        kf = _pad_rows(k_ref[0, rows, :], c).astype(F32)
        v = _pad_rows(v_ref[0, rows, :], c).astype(BF16)
        scores = _dot_nt(q, kf.astype(BF16)) * decay_ref[...]
        o = _dot(scores.astype(BF16), v) + _dot(q, state.astype(BF16)) * d_query
        state = d_chunk * state + _dot_tn((kf * d_key).astype(BF16), v)
        o = o[:chunk]
        ms = jnp.mean(o * o, axis=-1, keepdims=True)
        o_ref[0, rows, :] = (o * lax.rsqrt(ms + NORM_EPS) * _silu(g_ref[0, rows, :])).astype(o_ref.dtype)
    s_ref[0, 0] = state


def _retention(logg, q, k, v, g, s0, *, chunk, chunks_per_step, out_dtype):
    b, s, wqk = q.shape
    nh = logg.shape[0]
    dk = wqk // nh
    dv = v.shape[2] // nh
    rows = chunk * chunks_per_step
    blk = lambda wd: pl.BlockSpec((1, rows, wd), lambda bi, hi, ci: (bi, ci, hi))
    st_blk = pl.BlockSpec((1, 1, dk, dv), lambda bi, hi, ci: (bi, hi, 0, 0))
    in_specs = [_SMEM, blk(dk), blk(dk), blk(dv), blk(dv)]
    args = [logg, q, k, v, g]
    if s0 is not None:
        in_specs.append(st_blk)
        args.append(s0)
    kern = functools.partial(_retention_kernel, chunk=chunk, n_chunks=chunks_per_step,
                             has_init=s0 is not None)
    c = max(chunk, BF16_ROWS)
    return pl.pallas_call(
        kern,
        out_shape=[jax.ShapeDtypeStruct((b, s, nh * dv), out_dtype),
                   jax.ShapeDtypeStruct((b, nh, dk, dv), F32)],
        grid=(b, nh, s // rows),
        in_specs=in_specs,
        out_specs=[blk(dv), st_blk],
        scratch_shapes=[pltpu.VMEM((c, c), F32)],
        compiler_params=_cparams(("parallel", "parallel", "arbitrary")),
        name="retention",
    )(*args)


def _rope_tables(pos, half):
    inv_freq = 1.0 / (ROPE_BASE ** jnp.linspace(0.0, 1.0, half, dtype=F32))
    ang = pos.astype(F32)[:, None] * inv_freq[None, :]
    return jnp.cos(ang), jnp.sin(ang)


def kernel(x_prompt, x_sample, cache_k, cache_v, page_table, state_ret, norm_pre_attn, w_in_attn, lambda_q1, lambda_k1, lambda_q2, lambda_k2, subln_w_attn, w_out_attn, norm_post_attn, norm_pre_ret, w_in_ret, w_out_ret, norm_post_ret):
    b, s, d = x_prompt.shape
    db, t, _ = x_sample.shape
    nh_a = cache_k.shape[2]
    nh_r = state_ret.shape[1]
    dk_r, dv_r = state_ret.shape[2:]
    past = page_table.shape[1] * cache_k.shape[1]
    wa = nh_a * HEAD_W
    wqk_r, wv_r = nh_r * dk_r, nh_r * dv_r

    slopes = jnp.exp2(-8.0 * jnp.arange(1, nh_a + 1, dtype=F32) / nh_a)
    logg = jnp.log1p(-jnp.exp2(-5.0 - jnp.arange(nh_r, dtype=F32)))
    lams = (lambda_q1, lambda_k1, lambda_q2, lambda_k2)
    xp = x_prompt.reshape(b * s, d)
    xs = x_sample.reshape(db * t, d)

    segs_s = [(wa, [(F32, "q")]), (wa, [(F32, "k")]), (wa, [(F32, "v")]), (wa, [(F32, "g")])]
    qs, ks, vs, gs, *w_in_a = _norm_inproj(xs, norm_pre_attn, w_in_attn, segs_s, tm=1024,
                                           n_col_tiles=8, name="inproj_attn_sample")
    segs_p = [(wa, [(BF16, "q_log2")]), (wa, [(F32, "k"), (BF16, "kb")]),
              (wa, [(F32, "v"), (BF16, "vb")]), (wa, [(F32, "g")])]
    qp, kp, kbp, vp, vbp, gp = _norm_inproj(xp, norm_pre_attn, w_in_a, segs_p, tm=1024, n_col_tiles=8,
                                            name="inproj_attn_prompt")

    r3 = lambda a, n: a.reshape(n, -1, a.shape[-1])
    gated_p = _attn_prompt(slopes, r3(qp, b), r3(kbp, b), r3(vbp, b), r3(gp, b), subln_w_attn, lams,
                           tq=512, tk=512, tsub=128)
    gated_s = _attn_sample(page_table, slopes, r3(qs, db), ks.reshape(db, t, nh_a, HEAD_W),
                           vs.reshape(db, t, nh_a, HEAD_W), r3(gs, db), cache_k, cache_v,
                           subln_w_attn, lams, pages_per_step=4)

    y1s, w_out_a = _outproj_norm_residual_cast(gated_s.reshape(db * t, wa), w_out_attn, xs,
                                               norm_post_attn, tk=512, name="outproj_attn_sample")
    y1p = _outproj_norm_residual(gated_p.reshape(b * s, wa), w_out_a, xp, norm_post_attn,
                                 tm=512, tsub=128, name="outproj_attn_prompt")

    cos_p, sin_p = _rope_tables(jnp.arange(s, dtype=jnp.int32), dk_r // 2)
    cos_s, sin_s = _rope_tables(past + jnp.arange(t, dtype=jnp.int32), dk_r // 2)
    cos_s, sin_s = jnp.tile(cos_s, (db, 1)), jnp.tile(sin_s, (db, 1))
    rs = (1.0, dk_r ** -0.5)
    segs_rs = [(wqk_r, [(F32, "rot_q")]), (wqk_r, [(F32, "rot_k")]),
               (wv_r, [(F32, "v")]), (wv_r, [(F32, "g")])]
    sq, sk, sv, sg, *w_in_r = _norm_inproj(y1s, norm_pre_ret, w_in_ret, segs_rs, tm=1024, n_col_tiles=8,
                                           cos=cos_s, sin=sin_s, rotate_scale=rs,
                                           name="inproj_ret_sample")
    segs_rp = [(wqk_r, [(BF16, "rot_q")]), (wqk_r, [(BF16, "rot_k")]),
               (wv_r, [(BF16, "v")]), (wv_r, [(F32, "g")])]
    rq, rk, rv, rg = _norm_inproj(y1p, norm_pre_ret, w_in_r, segs_rp, tm=1024, n_col_tiles=8,
                                  cos=cos_p, sin=sin_p, rotate_scale=rs, name="inproj_ret_prompt")

    gated_rp, state_p = _retention(logg, r3(rq, b), r3(rk, b), r3(rv, b), r3(rg, b), None,
                                   chunk=256, chunks_per_step=4, out_dtype=BF16)
    gated_rs, state_s = _retention(logg, r3(sq, db), r3(sk, db), r3(sv, db), r3(sg, db),
                                   state_ret.astype(F32), chunk=t, chunks_per_step=1, out_dtype=F32)

    y2s, w_out_r = _outproj_norm_residual_cast(gated_rs.reshape(db * t, wv_r), w_out_ret, y1s,
                                               norm_post_ret, tk=512, name="outproj_ret_sample")
    y2p = _outproj_norm_residual(gated_rp.reshape(b * s, wv_r), w_out_r, y1p, norm_post_ret,
                                 tm=512, tsub=128, name="outproj_ret_prompt")

    return (y2p.reshape(b, s, d), y2s.reshape(db, t, d),
            kp.reshape(b, s, nh_a, HEAD_W), vp.reshape(b, s, nh_a, HEAD_W), state_p,
            ks.reshape(db, t, nh_a, HEAD_W), vs.reshape(db, t, nh_a, HEAD_W), state_s)
```

```python
import functools
import math

import jax
import jax.numpy as jnp
from jax import lax
from jax.experimental import pallas as pl
from jax.experimental.pallas import tpu as pltpu

F32 = jnp.float32
BF16 = jnp.bfloat16

NORM_EPS = 1e-6
LAMBDA_INIT = 0.8 - 0.6 * math.exp(-0.3 * 0)
ROPE_BASE = 10000.0
LOG2E = math.log2(math.e)
HEAD_W = 256
HALF = 128
ATTN_Q_SCALE = HALF ** -0.5 * LOG2E
BF16_ROWS = 16
VMEM_LIMIT_BYTES = 56 * 1024 * 1024

_SMEM = pl.BlockSpec(memory_space=pltpu.SMEM)


def _cparams(sem):
    return pltpu.CompilerParams(dimension_semantics=sem, vmem_limit_bytes=VMEM_LIMIT_BYTES)


def _silu(g):
    return g * (1.0 / (1.0 + jnp.exp(-g)))


def _dot_nt(a, b):
    return lax.dot_general(a, b, (((1,), (1,)), ((), ())), preferred_element_type=F32)


def _dot_tn(a, b):
    return lax.dot_general(a, b, (((0,), (0,)), ((), ())), preferred_element_type=F32)


def _dot(a, b):
    return jnp.dot(a, b, preferred_element_type=F32)


def _pad_rows(x, n):
    if x.shape[0] == n:
        return x
    return jnp.concatenate([x, jnp.zeros((n - x.shape[0], x.shape[1]), x.dtype)], axis=0)


def _lane_tile(x, width):
    return jnp.concatenate([x] * (width // x.shape[1]), axis=-1)


def _inproj_kernel(*refs, seg_kinds, rotate_scale, emit_bf16_weights, n_order_only):
    nseg = len(seg_kinds)
    x_ref, nw_ref = refs[:2]
    w_refs = refs[2:2 + nseg]
    n_in = 2 + nseg
    cos_ref = sin_ref = None
    if rotate_scale is not None:
        cos_ref, sin_ref = refs[n_in:n_in + 2]
        n_in += 2
    n_in += n_order_only
    h_ref = refs[-1]
    out_refs = refs[n_in:-1]
    if emit_bf16_weights:
        out_refs, wb_refs = out_refs[:-nseg], out_refs[-nseg:]
    else:
        wb_refs = (None,) * nseg

    @pl.when(pl.program_id(1) == 0)
    def _():
        x = x_ref[...]
        ms = jnp.mean(x * x, axis=-1, keepdims=True)
        h_ref[...] = (x * lax.rsqrt(ms + NORM_EPS) * nw_ref[...]).astype(BF16)

    def rotated(a, scale):
        cos = cos_ref[...]
        sin = sin_ref[...]
        parts = []
        for hh in range(a.shape[1] // HEAD_W):
            x1 = a[:, hh * HEAD_W: hh * HEAD_W + HALF]
            x2 = a[:, hh * HEAD_W + HALF: (hh + 1) * HEAD_W]
            parts.append((x1 * cos - x2 * sin) * scale)
            parts.append((x1 * sin + x2 * cos) * scale)
        return jnp.concatenate(parts, axis=-1)

    oi = 0
    for w_ref, wb_ref, kinds in zip(w_refs, wb_refs, seg_kinds):
        w = w_ref[...]
        if wb_ref is not None:
            w = w.astype(BF16)
            wb_ref[...] = w
        acc = _dot(h_ref[...], w)
        for kind in kinds:
            ref = out_refs[oi]
            oi += 1
            if kind == "rot_q":
                ref[...] = rotated(acc, rotate_scale[0]).astype(ref.dtype)
            elif kind == "rot_k":
                ref[...] = rotated(acc, rotate_scale[1]).astype(ref.dtype)
            elif kind == "q_log2":
                ref[...] = (acc * ATTN_Q_SCALE).astype(ref.dtype)
            else:
                ref[...] = acc.astype(ref.dtype)


def _norm_inproj(x, nw, w, segs, *, tm, n_col_tiles, cos=None, sin=None, rotate_scale=None,
                 run_after=(), name):
    m, d = x.shape
    tm = min(tm, m)
    nj = n_col_tiles
    emit = not isinstance(w, (list, tuple))
    assert not emit or m == tm
    seg_kinds = tuple(tuple(kind for _, kind in outs) for _, outs in segs)
    in_specs = [
        pl.BlockSpec((tm, d), lambda i, j: (i, 0)),
        pl.BlockSpec((1, d), lambda i, j: (0, 0)),
    ]
    args = [x, nw.reshape(1, d)]
    out_shapes = []
    out_specs = []
    wb_shapes = []
    wb_specs = []
    off = 0
    for si, (width, outs) in enumerate(segs):
        tw = width // nj
        assert tw * nj == width and off % tw == 0 and tw % HEAD_W == 0
        if emit:
            in_specs.append(pl.BlockSpec((d, tw), lambda i, j, o=off // tw: (0, o + j)))
            args.append(w)
            wb_shapes.append(jax.ShapeDtypeStruct((d, width), BF16))
            wb_specs.append(pl.BlockSpec((d, tw), lambda i, j: (0, j)))
        else:
            assert w[si].shape == (d, width)
            in_specs.append(pl.BlockSpec((d, tw), lambda i, j: (0, j)))
            args.append(w[si])
        for dtype, _ in outs:
            out_shapes.append(jax.ShapeDtypeStruct((m, width), dtype))
            out_specs.append(pl.BlockSpec((tm, tw), lambda i, j: (i, j)))
        off += width
    assert not emit or off == w.shape[1]
    if rotate_scale is not None:
        nblk = cos.shape[0] // tm
        in_specs += [pl.BlockSpec((tm, HALF), lambda i, j: (i % nblk, 0))] * 2
        args += [cos, sin]
    in_specs += [pl.BlockSpec(memory_space=pl.ANY)] * len(run_after)
    args += list(run_after)
    out_shapes += wb_shapes
    out_specs += wb_specs
    kern = functools.partial(_inproj_kernel, seg_kinds=seg_kinds, rotate_scale=rotate_scale,
                             emit_bf16_weights=emit, n_order_only=len(run_after))
    return pl.pallas_call(
        kern,
        out_shape=out_shapes,
        grid=(m // tm, nj),
        in_specs=in_specs,
        out_specs=out_specs,
        scratch_shapes=[pltpu.VMEM((tm, d), BF16)],
        compiler_params=_cparams(("parallel", "arbitrary")),
        name=name,
    )(*args)


def _outproj_kernel(a_ref, w_ref, x_ref, nw_ref, o_ref, *, tsub):
    for r0 in range(0, a_ref.shape[0], tsub):
        rows = pl.ds(r0, tsub)
        z = _dot(a_ref[rows, :].astype(BF16), w_ref[...])
        ms = jnp.mean(z * z, axis=-1, keepdims=True)
        o_ref[rows, :] = x_ref[rows, :] + z * lax.rsqrt(ms + NORM_EPS) * nw_ref[...]


def _outproj_norm_residual(a, w, x, nw, *, tm, tsub, name):
    m, kd = a.shape
    d = w.shape[1]
    tm = min(tm, m)
    tsub = min(tsub, tm)
    return pl.pallas_call(
        functools.partial(_outproj_kernel, tsub=tsub),
        out_shape=jax.ShapeDtypeStruct((m, d), F32),
        grid=(m // tm,),
        in_specs=[
            pl.BlockSpec((tm, kd), lambda i: (i, 0)),
            pl.BlockSpec((kd, d), lambda i: (0, 0), pipeline_mode=pl.Buffered(1)),
            pl.BlockSpec((tm, d), lambda i: (i, 0)),
            pl.BlockSpec((1, d), lambda i: (0, 0)),
        ],
        out_specs=pl.BlockSpec((tm, d), lambda i: (i, 0)),
        compiler_params=_cparams(("parallel",)),
        name=name,
    )(a, w, x, nw.reshape(1, d))


def _outproj_cast_kernel(a_ref, w_ref, x_ref, nw_ref, o_ref, wb_ref, acc_ref):
    kk = pl.program_id(0)

    @pl.when(kk == 0)
    def _():
        acc_ref[...] = jnp.zeros_like(acc_ref)

    w = w_ref[...].astype(BF16)
    wb_ref[...] = w
    acc_ref[...] += _dot(a_ref[...].astype(BF16), w)

    @pl.when(kk == pl.num_programs(0) - 1)
    def _():
        z = acc_ref[...]
        ms = jnp.mean(z * z, axis=-1, keepdims=True)
        o_ref[...] = x_ref[...] + z * lax.rsqrt(ms + NORM_EPS) * nw_ref[...]


def _outproj_norm_residual_cast(a, w, x, nw, *, tk, name):
    m, kd = a.shape
    d = w.shape[1]
    return pl.pallas_call(
        _outproj_cast_kernel,
        out_shape=[jax.ShapeDtypeStruct((m, d), F32), jax.ShapeDtypeStruct((kd, d), BF16)],
        grid=(kd // tk,),
        in_specs=[
            pl.BlockSpec((m, tk), lambda k: (0, k)),
            pl.BlockSpec((tk, d), lambda k: (k, 0)),
            pl.BlockSpec((m, d), lambda k: (0, 0)),
            pl.BlockSpec((1, d), lambda k: (0, 0)),
        ],
        out_specs=[pl.BlockSpec((m, d), lambda k: (0, 0)), pl.BlockSpec((tk, d), lambda k: (k, 0))],
        scratch_shapes=[pltpu.VMEM((m, d), F32)],
        compiler_params=_cparams(("arbitrary",)),
        name=name,
    )(a, w, x, nw.reshape(1, d))


def _diff_lambda(lq1_ref, lk1_ref, lq2_ref, lk2_ref):
    a = jnp.sum(lq1_ref[...] * lk1_ref[...], axis=-1, keepdims=True)
    b = jnp.sum(lq2_ref[...] * lk2_ref[...], axis=-1, keepdims=True)
    return jnp.exp(a) - jnp.exp(b) + LAMBDA_INIT


def _subln_gate(o, g, subln):
    ms = jnp.mean(o * o, axis=-1, keepdims=True)
    of = o * lax.rsqrt(ms + NORM_EPS) * subln * (1.0 - LAMBDA_INIT)
    return of * _silu(g)


def _attn_prompt_kernel(slopes_ref, q_ref, k_ref, v_ref, g_ref, subln_ref,
                        lq1_ref, lk1_ref, lq2_ref, lk2_ref, o_ref, m_ref, l_ref, acc_ref,
                        *, tq, tk, tsub):
    h = pl.program_id(1)
    qi = pl.program_id(2)
    slope2 = slopes_ref[h] * LOG2E
    q0 = qi * tq
    m_ref[...] = jnp.full(m_ref.shape, -jnp.inf, F32)
    l_ref[...] = jnp.zeros(l_ref.shape, F32)
    acc_ref[...] = jnp.zeros(acc_ref.shape, F32)

    def kv_step(j, carry, diagonal):
        k0 = pl.multiple_of(j * tk, tk)
        col = lax.broadcasted_iota(jnp.int32, (1, tk), 1) + (k0 - q0)
        bias = slope2 * col.astype(F32)
        vblk = v_ref[0, pl.ds(k0, tk), :]
        for c in range(2):
            kc = k_ref[0, pl.ds(k0, tk), c * HALF:(c + 1) * HALF]
            for r0 in range(0, tq, tsub):
                rows = pl.ds(r0, tsub)
                s = _dot_nt(q_ref[0, rows, c * HALF:(c + 1) * HALF], kc) + bias
                if diagonal:
                    row = lax.broadcasted_iota(jnp.int32, (tsub, 1), 0) + r0
                    s = jnp.where(row >= col, s, -jnp.inf)
                m_old = m_ref[c, rows, :]
                m_new = jnp.maximum(m_old, jnp.max(s, axis=-1, keepdims=True))
                alpha = jnp.exp2(m_old - m_new)
                p = jnp.exp2(s - _lane_tile(m_new, tk))
                m_ref[c, rows, :] = m_new
                l_ref[c, rows, :] = alpha * l_ref[c, rows, :] + jnp.sum(p, axis=-1, keepdims=True)
                acc_ref[c, rows, :] = (_lane_tile(alpha, HEAD_W) * acc_ref[c, rows, :]
                                       + _dot(p.astype(BF16), vblk))
        return carry

    lax.fori_loop(0, qi, functools.partial(kv_step, diagonal=False), 0)
    kv_step(qi, 0, diagonal=True)

    lam = _diff_lambda(lq1_ref, lk1_ref, lq2_ref, lk2_ref)
    o = (acc_ref[0] / _lane_tile(l_ref[0], HEAD_W)
         - lam * (acc_ref[1] / _lane_tile(l_ref[1], HEAD_W)))
    o_ref[0] = _subln_gate(o, g_ref[0], subln_ref[...]).astype(o_ref.dtype)


def _attn_prompt(slopes, q, kb, vb, g, subln, lams, *, tq, tk, tsub):
    b, s, w = q.shape
    nh = w // HEAD_W
    blk_q = pl.BlockSpec((1, tq, HEAD_W), lambda bi, hi, qi: (bi, qi, hi))
    blk_kv = pl.BlockSpec((1, s, HEAD_W), lambda bi, hi, qi: (bi, 0, hi))
    vec = lambda n: pl.BlockSpec((1, n), lambda bi, hi, qi: (0, 0))
    assert tq == tk and tq % tsub == 0
    kern = functools.partial(_attn_prompt_kernel, tq=tq, tk=tk, tsub=tsub)
    return pl.pallas_call(
        kern,
        out_shape=jax.ShapeDtypeStruct((b, s, w), BF16),
        grid=(b, nh, s // tq),
        in_specs=[_SMEM, blk_q, blk_kv, blk_kv, blk_q, vec(HEAD_W)] + [vec(HALF)] * 4,
        out_specs=blk_q,
        scratch_shapes=[pltpu.VMEM((2, tq, HALF), F32), pltpu.VMEM((2, tq, HALF), F32),
                        pltpu.VMEM((2, tq, HEAD_W), F32)],
        compiler_params=_cparams(("parallel", "parallel", "arbitrary")),
        name="diff_attn_prompt",
    )(slopes, q, kb, vb, g, subln.reshape(1, HEAD_W), *[v.reshape(1, HALF) for v in lams])


def _attn_sample_kernel(pt_ref, slopes_ref, q_ref, knew_ref, vnew_ref, g_ref, subln_ref,
                        lq1_ref, lk1_ref, lq2_ref, lk2_ref, *rest,
                        pages_per_step, n_steps, nh, t, page, past):
    del pt_ref
    pp = pages_per_step
    k_refs = rest[:pp]
    v_refs = rest[pp:2 * pp]
    o_ref = rest[2 * pp]
    wq_ref, bias_ref, sl_ref, m_ref, l_ref, acc_ref = rest[2 * pp + 1:]
    step = pl.program_id(1)
    scale = HALF ** -0.5
    nrow = nh * 2 * t
    lanes = page * nh
    log2 = lambda n: n.bit_length() - 1

    def row_head_tok():
        r = lax.broadcasted_iota(jnp.int32, (nrow, 1), 0)
        return jnp.right_shift(r, log2(2 * t)), jnp.bitwise_and(r, t - 1)

    def col_key_head(n):
        ln = lax.broadcasted_iota(jnp.int32, (1, n), 1)
        return jnp.right_shift(ln, log2(nh)), jnp.bitwise_and(ln, nh - 1)

    def slope_rows():
        rh, _ = row_head_tok()
        sl = jnp.zeros((nrow, 1), F32)
        for h in range(nh):
            sl = jnp.where(rh == h, slopes_ref[h], sl)
        return sl

    def alibi(dist_key0, n, causal):
        rh, rt = row_head_tok()
        key, ch = col_key_head(n)
        dist = dist_key0 + rt - key
        ok = ch == rh
        if causal:
            ok = ok & (dist >= 0)
        return jnp.where(ok, -slope_rows() * dist.astype(F32), -jnp.inf)

    @pl.when(step == 0)
    def _():
        q = q_ref[...]
        z = jnp.zeros((t, HALF), F32)
        for h in range(nh):
            q1 = q[:, h * HEAD_W: h * HEAD_W + HALF]
            q2 = q[:, h * HEAD_W + HALF: (h + 1) * HEAD_W]
            top = jnp.concatenate([q1, z], axis=-1)
            bot = jnp.concatenate([z, q2], axis=-1)
            wq_ref[h * 2 * t:(h + 1) * 2 * t, :] = jnp.concatenate([top, bot], axis=0).astype(BF16)
        bias_ref[...] = alibi(past, lanes, causal=False)
        sl_ref[...] = jnp.broadcast_to(slope_rows(), sl_ref.shape)
        m_ref[...] = jnp.full(m_ref.shape, -jnp.inf, F32)
        l_ref[...] = jnp.zeros(l_ref.shape, F32)
        acc_ref[...] = jnp.zeros(acc_ref.shape, F32)

    def update(k2, v2, bias, row_shift):
        st = _dot_nt(wq_ref[...], k2.astype(BF16)) * scale + bias
        m_old = m_ref[...]
        m_new = jnp.maximum(m_old, jnp.max(st, axis=-1, keepdims=True) + row_shift)
        alpha = jnp.exp(m_old - m_new)
        p = jnp.exp(st + _lane_tile(row_shift - m_new, st.shape[1]))
        m_ref[...] = m_new
        l_ref[...] = alpha * l_ref[...] + jnp.sum(p, axis=-1, keepdims=True)
        acc_ref[...] = (_lane_tile(alpha, HEAD_W) * acc_ref[...]
                        + _dot(p.astype(BF16), v2.astype(BF16)))

    for r in range(pp):
        pg = step * pp + r
        update(k_refs[r][...], v_refs[r][...], bias_ref[...],
               sl_ref[...] * (pg * page).astype(F32))

    @pl.when(step == n_steps - 1)
    def _():
        n_new = HALF
        update(_pad_rows(knew_ref[...], n_new), _pad_rows(vnew_ref[...], n_new),
               alibi(0, n_new, causal=True), jnp.zeros(sl_ref.shape, F32))
        lam = _diff_lambda(lq1_ref, lk1_ref, lq2_ref, lk2_ref)
        on = acc_ref[...] / _lane_tile(l_ref[...], HEAD_W)
        g = g_ref[...]
        for h in range(nh):
            o = on[h * 2 * t: h * 2 * t + t] - lam * on[h * 2 * t + t: (h + 1) * 2 * t]
            o_ref[:, h * HEAD_W:(h + 1) * HEAD_W] = _subln_gate(
                o, g[:, h * HEAD_W:(h + 1) * HEAD_W], subln_ref[...])


def _attn_sample(page_table, slopes, q, k_new, v_new, g, cache_k, cache_v, subln, lams,
                 *, pages_per_step):
    db, t, w = q.shape
    nh = w // HEAD_W
    n_phys, page = cache_k.shape[:2]
    n_pages = page_table.shape[1]
    past = n_pages * page
    pp = pages_per_step
    n_steps = n_pages // pp
    nrow = nh * 2 * t
    assert n_steps * pp == n_pages and nrow % BF16_ROWS == 0
    assert nh & (nh - 1) == 0 and t & (t - 1) == 0 and t * nh <= HALF
    ck = cache_k.reshape(n_phys, page * nh, HEAD_W)
    cv = cache_v.reshape(n_phys, page * nh, HEAD_W)
    kn = k_new.reshape(db, t * nh, HEAD_W)
    vn = v_new.reshape(db, t * nh, HEAD_W)

    def page_map(r):
        return lambda b, s, pt: (pt[b, s * pp + r], 0, 0)

    per_b = lambda shape: pl.BlockSpec((None,) + shape, lambda b, s, pt: (b, 0, 0))
    vec = lambda n: pl.BlockSpec((1, n), lambda b, s, pt: (0, 0))
    page_blk = (None, page * nh, HEAD_W)
    page_specs = [pl.BlockSpec(page_blk, page_map(r)) for r in range(pp)]
    in_specs = ([_SMEM, per_b((t, w)), per_b((t * nh, HEAD_W)), per_b((t * nh, HEAD_W)),
                 per_b((t, w)), vec(HEAD_W)] + [vec(HALF)] * 4 + page_specs * 2)
    kern = functools.partial(_attn_sample_kernel, pages_per_step=pp, n_steps=n_steps, nh=nh, t=t,
                             page=page, past=past)
    return pl.pallas_call(
        kern,
        out_shape=jax.ShapeDtypeStruct((db, t, w), F32),
        grid_spec=pltpu.PrefetchScalarGridSpec(
            num_scalar_prefetch=1,
            grid=(db, n_steps),
            in_specs=in_specs,
            out_specs=per_b((t, w)),
            scratch_shapes=[
                pltpu.VMEM((nrow, HEAD_W), BF16),
                pltpu.VMEM((nrow, page * nh), F32),
                pltpu.VMEM((nrow, HALF), F32),
                pltpu.VMEM((nrow, HALF), F32),
                pltpu.VMEM((nrow, HALF), F32),
                pltpu.VMEM((nrow, HEAD_W), F32),
            ],
        ),
        compiler_params=_cparams(("parallel", "arbitrary")),
        name="diff_attn_sample",
    )(page_table, slopes, q, kn, vn, g, subln.reshape(1, HEAD_W),
      *[v.reshape(1, HALF) for v in lams], *([ck] * pp), *([cv] * pp))


def _retention_kernel(logg_ref, q_ref, k_ref, v_ref, g_ref, *rest, chunk, n_chunks, has_init):
    if has_init:
        s0_ref, o_ref, s_ref, decay_ref = rest
    else:
        s0_ref = None
        o_ref, s_ref, decay_ref = rest
    h = pl.program_id(1)
    lg = jnp.full((1, 1), logg_ref[h], F32)
    c = max(chunk, BF16_ROWS)
    ri = lax.broadcasted_iota(jnp.int32, (c, 1), 0)
    rif = ri.astype(F32)

    @pl.when(pl.program_id(2) == 0)
    def _():
        if has_init:
            s_ref[0, 0] = s0_ref[0, 0]
        else:
            s_ref[0, 0] = jnp.zeros(s_ref.shape[2:], F32)
        diff = ri - lax.broadcasted_iota(jnp.int32, (1, c), 1)
        decay_ref[...] = jnp.where(diff >= 0, jnp.exp(lg * jnp.maximum(diff, 0).astype(F32)), 0.0)

    d_query = jnp.exp(lg * (rif + 1.0))
    d_key = jnp.exp(lg * (chunk - 1.0 - rif))
    d_chunk = jnp.exp(lg * float(chunk))
    state = s_ref[0, 0]
    for n in range(n_chunks):
        rows = pl.ds(n * chunk, chunk)
        q = _pad_rows(q_ref[0, rows, :], c).astype(BF16)
        kf = _pad_rows(k_ref[0, rows, :], c).astype(F32)
        v = _pad_rows(v_ref[0, rows, :], c).astype(BF16)
        scores = _dot_nt(q, kf.astype(BF16)) * decay_ref[...]
        o = _dot(scores.astype(BF16), v) + _dot(q, state.astype(BF16)) * d_query
        state = d_chunk * state + _dot_tn((kf * d_key).astype(BF16), v)
        o = o[:chunk]
        ms = jnp.mean(o * o, axis=-1, keepdims=True)
        o_ref[0, rows, :] = (o * lax.rsqrt(ms + NORM_EPS) * _silu(g_ref[0, rows, :])).astype(o_ref.dtype)
    s_ref[0, 0] = state


def _retention(logg, q, k, v, g, s0, *, chunk, chunks_per_step, out_dtype):
    b, s, wqk = q.shape
    nh = logg.shape[0]
    dk = wqk // nh
    dv = v.shape[2] // nh
    rows = chunk * chunks_per_step
    blk = lambda wd: pl.BlockSpec((1, rows, wd), lambda bi, hi, ci: (bi, ci, hi))
    st_blk = pl.BlockSpec((1, 1, dk, dv), lambda bi, hi, ci: (bi, hi, 0, 0))
    in_specs = [_SMEM, blk(dk), blk(dk), blk(dv), blk(dv)]
    args = [logg, q, k, v, g]
    if s0 is not None:
        in_specs.append(st_blk)
        args.append(s0)
    kern = functools.partial(_retention_kernel, chunk=chunk, n_chunks=chunks_per_step,
                             has_init=s0 is not None)
    c = max(chunk, BF16_ROWS)
    return pl.pallas_call(
        kern,
        out_shape=[jax.ShapeDtypeStruct((b, s, nh * dv), out_dtype),
                   jax.ShapeDtypeStruct((b, nh, dk, dv), F32)],
        grid=(b, nh, s // rows),
        in_specs=in_specs,
        out_specs=[blk(dv), st_blk],
        scratch_shapes=[pltpu.VMEM((c, c), F32)],
        compiler_params=_cparams(("parallel", "parallel", "arbitrary")),
        name="retention",
    )(*args)


def _rope_tables(pos, half):
    inv_freq = 1.0 / (ROPE_BASE ** jnp.linspace(0.0, 1.0, half, dtype=F32))
    ang = pos.astype(F32)[:, None] * inv_freq[None, :]
    return jnp.cos(ang), jnp.sin(ang)


def kernel(x_prompt, x_sample, cache_k, cache_v, page_table, state_ret, norm_pre_attn, w_in_attn, lambda_q1, lambda_k1, lambda_q2, lambda_k2, subln_w_attn, w_out_attn, norm_post_attn, norm_pre_ret, w_in_ret, w_out_ret, norm_post_ret):
    b, s, d = x_prompt.shape
    db, t, _ = x_sample.shape
    nh_a = cache_k.shape[2]
    nh_r = state_ret.shape[1]
    dk_r, dv_r = state_ret.shape[2:]
    past = page_table.shape[1] * cache_k.shape[1]
    wa = nh_a * HEAD_W
    wqk_r, wv_r = nh_r * dk_r, nh_r * dv_r

    slopes = jnp.exp2(-8.0 * jnp.arange(1, nh_a + 1, dtype=F32) / nh_a)
    logg = jnp.log1p(-jnp.exp2(-5.0 - jnp.arange(nh_r, dtype=F32)))
    lams = (lambda_q1, lambda_k1, lambda_q2, lambda_k2)
    xp = x_prompt.reshape(b * s, d)
    xs = x_sample.reshape(db * t, d)

    segs_s = [(wa, [(F32, "q")]), (wa, [(F32, "k")]), (wa, [(F32, "v")]), (wa, [(F32, "g")])]
    qs, ks, vs, gs, *w_in_a = _norm_inproj(xs, norm_pre_attn, w_in_attn, segs_s, tm=1024,
                                           n_col_tiles=8, name="inproj_attn_sample")
    r3 = lambda a, n: a.reshape(n, -1, a.shape[-1])
    gated_s = _attn_sample(page_table, slopes, r3(qs, db), ks.reshape(db, t, nh_a, HEAD_W),
                           vs.reshape(db, t, nh_a, HEAD_W), r3(gs, db), cache_k, cache_v,
                           subln_w_attn, lams, pages_per_step=4)
    y1s, w_out_a = _outproj_norm_residual_cast(gated_s.reshape(db * t, wa), w_out_attn, xs,
                                               norm_post_attn, tk=512, name="outproj_attn_sample")

    segs_p = [(wa, [(BF16, "q_log2")]), (wa, [(F32, "k"), (BF16, "kb")]),
              (wa, [(F32, "v"), (BF16, "vb")]), (wa, [(F32, "g")])]
    qp, kp, kbp, vp, vbp, gp = _norm_inproj(xp, norm_pre_attn, w_in_a, segs_p, tm=1024, n_col_tiles=8,
                                            run_after=(y1s,), name="inproj_attn_prompt")
    gated_p = _attn_prompt(slopes, r3(qp, b), r3(kbp, b), r3(vbp, b), r3(gp, b), subln_w_attn, lams,
                           tq=512, tk=512, tsub=128)
    y1p = _outproj_norm_residual(gated_p.reshape(b * s, wa), w_out_a, xp, norm_post_attn,
                                 tm=512, tsub=128, name="outproj_attn_prompt")

    cos_p, sin_p = _rope_tables(jnp.arange(s, dtype=jnp.int32), dk_r // 2)
    cos_s, sin_s = _rope_tables(past + jnp.arange(t, dtype=jnp.int32), dk_r // 2)
    cos_s, sin_s = jnp.tile(cos_s, (db, 1)), jnp.tile(sin_s, (db, 1))
    rs = (1.0, dk_r ** -0.5)
    segs_rs = [(wqk_r, [(F32, "rot_q")]), (wqk_r, [(F32, "rot_k")]),
               (wv_r, [(F32, "v")]), (wv_r, [(F32, "g")])]
    sq, sk, sv, sg, *w_in_r = _norm_inproj(y1s, norm_pre_ret, w_in_ret, segs_rs, tm=1024, n_col_tiles=8,
                                           cos=cos_s, sin=sin_s, rotate_scale=rs,
                                           name="inproj_ret_sample")
    segs_rp = [(wqk_r, [(BF16, "rot_q")]), (wqk_r, [(BF16, "rot_k")]),
               (wv_r, [(BF16, "v")]), (wv_r, [(F32, "g")])]
    rq, rk, rv, rg = _norm_inproj(y1p, norm_pre_ret, w_in_r, segs_rp, tm=1024, n_col_tiles=8,
                                  cos=cos_p, sin=sin_p, rotate_scale=rs, name="inproj_ret_prompt")

    gated_rp, state_p = _retention(logg, r3(rq, b), r3(rk, b), r3(rv, b), r3(rg, b), None,
                                   chunk=256, chunks_per_step=4, out_dtype=BF16)
    gated_rs, state_s = _retention(logg, r3(sq, db), r3(sk, db), r3(sv, db), r3(sg, db),
                                   state_ret.astype(F32), chunk=t, chunks_per_step=1, out_dtype=F32)

    y2s, w_out_r = _outproj_norm_residual_cast(gated_rs.reshape(db * t, wv_r), w_out_ret, y1s,
                                               norm_post_ret, tk=512, name="outproj_ret_sample")
    y2p = _outproj_norm_residual(gated_rp.reshape(b * s, wv_r), w_out_r, y1p, norm_post_ret,
                                 tm=512, tsub=128, name="outproj_ret_prompt")

    return (y2p.reshape(b, s, d), y2s.reshape(db, t, d),
            kp.reshape(b, s, nh_a, HEAD_W), vp.reshape(b, s, nh_a, HEAD_W), state_p,
            ks.reshape(db, t, nh_a, HEAD_W), vs.reshape(db, t, nh_a, HEAD_W), state_s)
```

```python
import functools
import math

import jax
import jax.numpy as jnp
from jax import lax
from jax.experimental import pallas as pl
from jax.experimental.pallas import tpu as pltpu

F32 = jnp.float32
BF16 = jnp.bfloat16

NORM_EPS = 1e-6
LAMBDA_INIT = 0.8 - 0.6 * math.exp(-0.3 * 0)
ROPE_BASE = 10000.0
LOG2E = math.log2(math.e)
HEAD_W = 256
HALF = 128
ATTN_Q_SCALE = HALF ** -0.5 * LOG2E
BF16_ROWS = 16
VMEM_LIMIT_BYTES = 56 * 1024 * 1024

_SMEM = pl.BlockSpec(memory_space=pltpu.SMEM)


def _cparams(sem):
    return pltpu.CompilerParams(dimension_semantics=sem, vmem_limit_bytes=VMEM_LIMIT_BYTES)


def _silu(g):
    return g * (1.0 / (1.0 + jnp.exp(-g)))


def _dot_nt(a, b):
    return lax.dot_general(a, b, (((1,), (1,)), ((), ())), preferred_element_type=F32)


def _dot_tn(a, b):
    return lax.dot_general(a, b, (((0,), (0,)), ((), ())), preferred_element_type=F32)


def _dot(a, b):
    return jnp.dot(a, b, preferred_element_type=F32)


def _pad_rows(x, n):
    if x.shape[0] == n:
        return x
    return jnp.concatenate([x, jnp.zeros((n - x.shape[0], x.shape[1]), x.dtype)], axis=0)


def _lane_tile(x, width):
    return jnp.concatenate([x] * (width // x.shape[1]), axis=-1)


def _inproj_kernel(*refs, seg_kinds, rotate_scale, emit_bf16_weights, n_order_only):
    nseg = len(seg_kinds)
    x_ref, nw_ref = refs[:2]
    w_refs = refs[2:2 + nseg]
    n_in = 2 + nseg
    cos_ref = sin_ref = None
    if rotate_scale is not None:
        cos_ref, sin_ref = refs[n_in:n_in + 2]
        n_in += 2
    n_in += n_order_only
    h_ref = refs[-1]
    out_refs = refs[n_in:-1]
    if emit_bf16_weights:
        out_refs, wb_refs = out_refs[:-nseg], out_refs[-nseg:]
    else:
        wb_refs = (None,) * nseg

    @pl.when(pl.program_id(1) == 0)
    def _():
        x = x_ref[...]
        ms = jnp.mean(x * x, axis=-1, keepdims=True)
        h_ref[...] = (x * lax.rsqrt(ms + NORM_EPS) * nw_ref[...]).astype(BF16)

    def rotated(a, scale):
        cos = cos_ref[...]
        sin = sin_ref[...]
        parts = []
        for hh in range(a.shape[1] // HEAD_W):
            x1 = a[:, hh * HEAD_W: hh * HEAD_W + HALF]
            x2 = a[:, hh * HEAD_W + HALF: (hh + 1) * HEAD_W]
            parts.append((x1 * cos - x2 * sin) * scale)
            parts.append((x1 * sin + x2 * cos) * scale)
        return jnp.concatenate(parts, axis=-1)

    oi = 0
    for w_ref, wb_ref, kinds in zip(w_refs, wb_refs, seg_kinds):
        w = w_ref[...]
        if wb_ref is not None:
            w = w.astype(BF16)
            wb_ref[...] = w
        acc = _dot(h_ref[...], w)
        for kind in kinds:
            ref = out_refs[oi]
            oi += 1
            if kind == "rot_q":
                ref[...] = rotated(acc, rotate_scale[0]).astype(ref.dtype)
            elif kind == "rot_k":
                ref[...] = rotated(acc, rotate_scale[1]).astype(ref.dtype)
            elif kind == "q_log2":
                ref[...] = (acc * ATTN_Q_SCALE).astype(ref.dtype)
            else:
                ref[...] = acc.astype(ref.dtype)


def _norm_inproj(x, nw, w, segs, *, tm, n_col_tiles, cos=None, sin=None, rotate_scale=None,
                 run_after=(), name):
    m, d = x.shape
    tm = min(tm, m)
    nj = n_col_tiles
    emit = not isinstance(w, (list, tuple))
    assert not emit or m == tm
    seg_kinds = tuple(tuple(kind for _, kind in outs) for _, outs in segs)
    in_specs = [
        pl.BlockSpec((tm, d), lambda i, j: (i, 0)),
        pl.BlockSpec((1, d), lambda i, j: (0, 0)),
    ]
    args = [x, nw.reshape(1, d)]
    out_shapes = []
    out_specs = []
    wb_shapes = []
    wb_specs = []
    off = 0
    for si, (width, outs) in enumerate(segs):
        tw = width // nj
        assert tw * nj == width and off % tw == 0 and tw % HEAD_W == 0
        if emit:
            in_specs.append(pl.BlockSpec((d, tw), lambda i, j, o=off // tw: (0, o + j)))
            args.append(w)
            wb_shapes.append(jax.ShapeDtypeStruct((d, width), BF16))
            wb_specs.append(pl.BlockSpec((d, tw), lambda i, j: (0, j)))
        else:
            assert w[si].shape == (d, width)
            in_specs.append(pl.BlockSpec((d, tw), lambda i, j: (0, j)))
            args.append(w[si])
        for dtype, _ in outs:
            out_shapes.append(jax.ShapeDtypeStruct((m, width), dtype))
            out_specs.append(pl.BlockSpec((tm, tw), lambda i, j: (i, j)))
        off += width
    assert not emit or off == w.shape[1]
    if rotate_scale is not None:
        nblk = cos.shape[0] // tm
        in_specs += [pl.BlockSpec((tm, HALF), lambda i, j: (i % nblk, 0))] * 2
        args += [cos, sin]
    in_specs += [pl.BlockSpec(memory_space=pl.ANY)] * len(run_after)
    args += list(run_after)
    out_shapes += wb_shapes
    out_specs += wb_specs
    kern = functools.partial(_inproj_kernel, seg_kinds=seg_kinds, rotate_scale=rotate_scale,
                             emit_bf16_weights=emit, n_order_only=len(run_after))
    return pl.pallas_call(
        kern,
        out_shape=out_shapes,
        grid=(m // tm, nj),
        in_specs=in_specs,
        out_specs=out_specs,
        scratch_shapes=[pltpu.VMEM((tm, d), BF16)],
        compiler_params=_cparams(("parallel", "arbitrary")),
        name=name,
    )(*args)


def _outproj_kernel(a_ref, w_ref, x_ref, nw_ref, o_ref, *, tsub):
    for r0 in range(0, a_ref.shape[0], tsub):
        rows = pl.ds(r0, tsub)
        z = _dot(a_ref[rows, :].astype(BF16), w_ref[...])
        ms = jnp.mean(z * z, axis=-1, keepdims=True)
        o_ref[rows, :] = x_ref[rows, :] + z * lax.rsqrt(ms + NORM_EPS) * nw_ref[...]


def _outproj_norm_residual(a, w, x, nw, *, tm, tsub, name):
    m, kd = a.shape
    d = w.shape[1]
    tm = min(tm, m)
    tsub = min(tsub, tm)
    return pl.pallas_call(
        functools.partial(_outproj_kernel, tsub=tsub),
        out_shape=jax.ShapeDtypeStruct((m, d), F32),
        grid=(m // tm,),
        in_specs=[
            pl.BlockSpec((tm, kd), lambda i: (i, 0)),
            pl.BlockSpec((kd, d), lambda i: (0, 0), pipeline_mode=pl.Buffered(1)),
            pl.BlockSpec((tm, d), lambda i: (i, 0)),
            pl.BlockSpec((1, d), lambda i: (0, 0)),
        ],
        out_specs=pl.BlockSpec((tm, d), lambda i: (i, 0)),
        compiler_params=_cparams(("parallel",)),
        name=name,
    )(a, w, x, nw.reshape(1, d))


def _outproj_cast_kernel(a_ref, w_ref, x_ref, nw_ref, o_ref, wb_ref, acc_ref):
    kk = pl.program_id(0)

    @pl.when(kk == 0)
    def _():
        acc_ref[...] = jnp.zeros_like(acc_ref)

    w = w_ref[...].astype(BF16)
    wb_ref[...] = w
    acc_ref[...] += _dot(a_ref[...].astype(BF16), w)

    @pl.when(kk == pl.num_programs(0) - 1)
    def _():
        z = acc_ref[...]
        ms = jnp.mean(z * z, axis=-1, keepdims=True)
        o_ref[...] = x_ref[...] + z * lax.rsqrt(ms + NORM_EPS) * nw_ref[...]


def _outproj_norm_residual_cast(a, w, x, nw, *, tk, name):
    m, kd = a.shape
    d = w.shape[1]
    return pl.pallas_call(
        _outproj_cast_kernel,
        out_shape=[jax.ShapeDtypeStruct((m, d), F32), jax.ShapeDtypeStruct((kd, d), BF16)],
        grid=(kd // tk,),
        in_specs=[
            pl.BlockSpec((m, tk), lambda k: (0, k)),
            pl.BlockSpec((tk, d), lambda k: (k, 0)),
            pl.BlockSpec((m, d), lambda k: (0, 0)),
            pl.BlockSpec((1, d), lambda k: (0, 0)),
        ],
        out_specs=[pl.BlockSpec((m, d), lambda k: (0, 0)), pl.BlockSpec((tk, d), lambda k: (k, 0))],
        scratch_shapes=[pltpu.VMEM((m, d), F32)],
        compiler_params=_cparams(("arbitrary",)),
        name=name,
    )(a, w, x, nw.reshape(1, d))


def _diff_lambda(lq1_ref, lk1_ref, lq2_ref, lk2_ref):
    a = jnp.sum(lq1_ref[...] * lk1_ref[...], axis=-1, keepdims=True)
    b = jnp.sum(lq2_ref[...] * lk2_ref[...], axis=-1, keepdims=True)
    return jnp.exp(a) - jnp.exp(b) + LAMBDA_INIT


def _subln_gate(o, g, subln):
    ms = jnp.mean(o * o, axis=-1, keepdims=True)
    of = o * lax.rsqrt(ms + NORM_EPS) * subln * (1.0 - LAMBDA_INIT)
    return of * _silu(g)


def _attn_prompt_kernel(slopes_ref, q_ref, k_ref, v_ref, g_ref, subln_ref,
                        lq1_ref, lk1_ref, lq2_ref, lk2_ref, o_ref, m_ref, l_ref, acc_ref,
                        *, tq, tk, tsub):
    h = pl.program_id(1)
    qi = pl.program_id(2)
    slope2 = slopes_ref[h] * LOG2E
    q0 = qi * tq
    m_ref[...] = jnp.full(m_ref.shape, -jnp.inf, F32)
    l_ref[...] = jnp.zeros(l_ref.shape, F32)
    acc_ref[...] = jnp.zeros(acc_ref.shape, F32)

    def kv_step(j, carry, diagonal):
        k0 = pl.multiple_of(j * tk, tk)
        col = lax.broadcasted_iota(jnp.int32, (1, tk), 1) + (k0 - q0)
        bias = slope2 * col.astype(F32)
        vblk = v_ref[0, pl.ds(k0, tk), :]
        for c in range(2):
            kc = k_ref[0, pl.ds(k0, tk), c * HALF:(c + 1) * HALF]
            for r0 in range(0, tq, tsub):
                rows = pl.ds(r0, tsub)
                s = _dot_nt(q_ref[0, rows, c * HALF:(c + 1) * HALF], kc) + bias
                if diagonal:
                    row = lax.broadcasted_iota(jnp.int32, (tsub, 1), 0) + r0
                    s = jnp.where(row >= col, s, -jnp.inf)
                m_old = m_ref[c, rows, :]
                m_new = jnp.maximum(m_old, jnp.max(s, axis=-1, keepdims=True))
                alpha = jnp.exp2(m_old - m_new)
                p = jnp.exp2(s - _lane_tile(m_new, tk))
                m_ref[c, rows, :] = m_new
                l_ref[c, rows, :] = alpha * l_ref[c, rows, :] + jnp.sum(p, axis=-1, keepdims=True)
                acc_ref[c, rows, :] = (_lane_tile(alpha, HEAD_W) * acc_ref[c, rows, :]
                                       + _dot(p.astype(BF16), vblk))
        return carry

    lax.fori_loop(0, qi, functools.partial(kv_step, diagonal=False), 0)
    kv_step(qi, 0, diagonal=True)

    lam = _diff_lambda(lq1_ref, lk1_ref, lq2_ref, lk2_ref)
    o = (acc_ref[0] / _lane_tile(l_ref[0], HEAD_W)
         - lam * (acc_ref[1] / _lane_tile(l_ref[1], HEAD_W)))
    o_ref[0] = _subln_gate(o, g_ref[0], subln_ref[...]).astype(o_ref.dtype)


def _attn_prompt(slopes, q, kb, vb, g, subln, lams, *, tq, tk, tsub):
    b, s, w = q.shape
    nh = w // HEAD_W
    blk_q = pl.BlockSpec((1, tq, HEAD_W), lambda bi, hi, qi: (bi, qi, hi))
    blk_kv = pl.BlockSpec((1, s, HEAD_W), lambda bi, hi, qi: (bi, 0, hi))
    vec = lambda n: pl.BlockSpec((1, n), lambda bi, hi, qi: (0, 0))
    assert tq == tk and tq % tsub == 0
    kern = functools.partial(_attn_prompt_kernel, tq=tq, tk=tk, tsub=tsub)
    return pl.pallas_call(
        kern,
        out_shape=jax.ShapeDtypeStruct((b, s, w), BF16),
        grid=(b, nh, s // tq),
        in_specs=[_SMEM, blk_q, blk_kv, blk_kv, blk_q, vec(HEAD_W)] + [vec(HALF)] * 4,
        out_specs=blk_q,
        scratch_shapes=[pltpu.VMEM((2, tq, HALF), F32), pltpu.VMEM((2, tq, HALF), F32),
                        pltpu.VMEM((2, tq, HEAD_W), F32)],
        compiler_params=_cparams(("parallel", "parallel", "arbitrary")),
        name="diff_attn_prompt",
    )(slopes, q, kb, vb, g, subln.reshape(1, HEAD_W), *[v.reshape(1, HALF) for v in lams])


def _attn_sample_kernel(pt_ref, slopes_ref, q_ref, knew_ref, vnew_ref, g_ref, subln_ref,
                        lq1_ref, lk1_ref, lq2_ref, lk2_ref, *rest,
                        pages_per_step, n_steps, nh, t, page, past):
    del pt_ref
    pp = pages_per_step
    k_refs = rest[:pp]
    v_refs = rest[pp:2 * pp]
    o_ref = rest[2 * pp]
    wq_ref, bias_ref, sl_ref, m_ref, l_ref, acc_ref = rest[2 * pp + 1:]
    step = pl.program_id(1)
    scale = HALF ** -0.5
    nrow = nh * 2 * t
    lanes = page * nh
    log2 = lambda n: n.bit_length() - 1

    def row_head_tok():
        r = lax.broadcasted_iota(jnp.int32, (nrow, 1), 0)
        return jnp.right_shift(r, log2(2 * t)), jnp.bitwise_and(r, t - 1)

    def col_key_head(n):
        ln = lax.broadcasted_iota(jnp.int32, (1, n), 1)
        return jnp.right_shift(ln, log2(nh)), jnp.bitwise_and(ln, nh - 1)

    def slope_rows():
        rh, _ = row_head_tok()
        sl = jnp.zeros((nrow, 1), F32)
        for h in range(nh):
            sl = jnp.where(rh == h, slopes_ref[h], sl)
        return sl

    def alibi(dist_key0, n, causal):
        rh, rt = row_head_tok()
        key, ch = col_key_head(n)
        dist = dist_key0 + rt - key
        ok = ch == rh
        if causal:
            ok = ok & (dist >= 0)
        return jnp.where(ok, -slope_rows() * dist.astype(F32), -jnp.inf)

    @pl.when(step == 0)
    def _():
        q = q_ref[...]
        z = jnp.zeros((t, HALF), F32)
        for h in range(nh):
            q1 = q[:, h * HEAD_W: h * HEAD_W + HALF]
            q2 = q[:, h * HEAD_W + HALF: (h + 1) * HEAD_W]
            top = jnp.concatenate([q1, z], axis=-1)
            bot = jnp.concatenate([z, q2], axis=-1)
            wq_ref[h * 2 * t:(h + 1) * 2 * t, :] = jnp.concatenate([top, bot], axis=0).astype(BF16)
        bias_ref[...] = alibi(past, lanes, causal=False)
        sl_ref[...] = jnp.broadcast_to(slope_rows(), sl_ref.shape)
        m_ref[...] = jnp.full(m_ref.shape, -jnp.inf, F32)
        l_ref[...] = jnp.zeros(l_ref.shape, F32)
        acc_ref[...] = jnp.zeros(acc_ref.shape, F32)

    def update(k2, v2, bias, row_shift):
        st = _dot_nt(wq_ref[...], k2.astype(BF16)) * scale + bias
        m_old = m_ref[...]
        m_new = jnp.maximum(m_old, jnp.max(st, axis=-1, keepdims=True) + row_shift)
        alpha = jnp.exp(m_old - m_new)
        p = jnp.exp(st + _lane_tile(row_shift - m_new, st.shape[1]))
        m_ref[...] = m_new
        l_ref[...] = alpha * l_ref[...] + jnp.sum(p, axis=-1, keepdims=True)
        acc_ref[...] = (_lane_tile(alpha, HEAD_W) * acc_ref[...]
                        + _dot(p.astype(BF16), v2.astype(BF16)))

    for r in range(pp):
        pg = step * pp + r
        update(k_refs[r][...], v_refs[r][...], bias_ref[...],
               sl_ref[...] * (pg * page).astype(F32))

    @pl.when(step == n_steps - 1)
    def _():
        n_new = HALF
        update(_pad_rows(knew_ref[...], n_new), _pad_rows(vnew_ref[...], n_new),
               alibi(0, n_new, causal=True), jnp.zeros(sl_ref.shape, F32))
        lam = _diff_lambda(lq1_ref, lk1_ref, lq2_ref, lk2_ref)
        on = acc_ref[...] / _lane_tile(l_ref[...], HEAD_W)
        g = g_ref[...]
        for h in range(nh):
            o = on[h * 2 * t: h * 2 * t + t] - lam * on[h * 2 * t + t: (h + 1) * 2 * t]
            o_ref[:, h * HEAD_W:(h + 1) * HEAD_W] = _subln_gate(
                o, g[:, h * HEAD_W:(h + 1) * HEAD_W], subln_ref[...])


def _attn_sample(page_table, slopes, q, k_new, v_new, g, cache_k, cache_v, subln, lams,
                 *, pages_per_step):
    db, t, w = q.shape
    nh = w // HEAD_W
    n_phys, page = cache_k.shape[:2]
    n_pages = page_table.shape[1]
    past = n_pages * page
    pp = pages_per_step
    n_steps = n_pages // pp
    nrow = nh * 2 * t
    assert n_steps * pp == n_pages and nrow % BF16_ROWS == 0
    assert nh & (nh - 1) == 0 and t & (t - 1) == 0 and t * nh <= HALF
    ck = cache_k.reshape(n_phys, page * nh, HEAD_W)
    cv = cache_v.reshape(n_phys, page * nh, HEAD_W)
    kn = k_new.reshape(db, t * nh, HEAD_W)
    vn = v_new.reshape(db, t * nh, HEAD_W)

    def page_map(r):
        return lambda b, s, pt: (pt[b, s * pp + r], 0, 0)

    per_b = lambda shape: pl.BlockSpec((None,) + shape, lambda b, s, pt: (b, 0, 0))
    vec = lambda n: pl.BlockSpec((1, n), lambda b, s, pt: (0, 0))
    page_blk = (None, page * nh, HEAD_W)
    page_specs = [pl.BlockSpec(page_blk, page_map(r)) for r in range(pp)]
    in_specs = ([_SMEM, per_b((t, w)), per_b((t * nh, HEAD_W)), per_b((t * nh, HEAD_W)),
                 per_b((t, w)), vec(HEAD_W)] + [vec(HALF)] * 4 + page_specs * 2)
    kern = functools.partial(_attn_sample_kernel, pages_per_step=pp, n_steps=n_steps, nh=nh, t=t,
                             page=page, past=past)
    return pl.pallas_call(
        kern,
        out_shape=jax.ShapeDtypeStruct((db, t, w), F32),
        grid_spec=pltpu.PrefetchScalarGridSpec(
            num_scalar_prefetch=1,
            grid=(db, n_steps),
            in_specs=in_specs,
            out_specs=per_b((t, w)),
            scratch_shapes=[
                pltpu.VMEM((nrow, HEAD_W), BF16),
                pltpu.VMEM((nrow, page * nh), F32),
                pltpu.VMEM((nrow, HALF), F32),
                pltpu.VMEM((nrow, HALF), F32),
                pltpu.VMEM((nrow, HALF), F32),
                pltpu.VMEM((nrow, HEAD_W), F32),
            ],
        ),
        compiler_params=_cparams(("parallel", "arbitrary")),
        name="diff_attn_sample",
    )(page_table, slopes, q, kn, vn, g, subln.reshape(1, HEAD_W),
      *[v.reshape(1, HALF) for v in lams], *([ck] * pp), *([cv] * pp))


def _retention_kernel(logg_ref, q_ref, k_ref, v_ref, g_ref, *rest, chunk, n_chunks, has_init):
    if has_init:
        s0_ref, o_ref, s_ref, decay_ref = rest
    else:
        s0_ref = None
        o_ref, s_ref, decay_ref = rest
    h = pl.program_id(1)
    lg = jnp.full((1, 1), logg_ref[h], F32)
    c = max(chunk, BF16_ROWS)
    ri = lax.broadcasted_iota(jnp.int32, (c, 1), 0)
    rif = ri.astype(F32)

    @pl.when(pl.program_id(2) == 0)
    def _():
        if has_init:
            s_ref[0, 0] = s0_ref[0, 0]
        else:
            s_ref[0, 0] = jnp.zeros(s_ref.shape[2:], F32)
        diff = ri - lax.broadcasted_iota(jnp.int32, (1, c), 1)
        decay_ref[...] = jnp.where(diff >= 0, jnp.exp(lg * jnp.maximum(diff, 0).astype(F32)), 0.0)

    d_query = jnp.exp(lg * (rif + 1.0))
    d_key = jnp.exp(lg * (chunk - 1.0 - rif))
    d_chunk = jnp.exp(lg * float(chunk))
    state = s_ref[0, 0]
    for n in range(n_chunks):
        rows = pl.ds(n * chunk, chunk)
        q = _pad_rows(q_ref[0, rows, :], c).astype(BF16)
        kf = _pad_rows(k_ref[0, rows, :], c).astype(F32)
        v = _pad_rows(v_ref[0, rows, :], c).astype(BF16)
        scores = _dot_nt(q, kf.astype(BF16)) * decay_ref[...]
        o = _dot(scores.astype(BF16), v) + _dot(q, state.astype(BF16)) * d_query
        state = d_chunk * state + _dot_tn((kf * d_key).astype(BF16), v)
        o = o[:chunk]
        ms = jnp.mean(o * o, axis=-1, keepdims=True)
        o_ref[0, rows, :] = (o * lax.rsqrt(ms + NORM_EPS) * _silu(g_ref[0, rows, :])).astype(o_ref.dtype)
    s_ref[0, 0] = state


def _retention(logg, q, k, v, g, s0, *, chunk, chunks_per_step, out_dtype):
    b, s, wqk = q.shape
    nh = logg.shape[0]
    dk = wqk // nh
    dv = v.shape[2] // nh
    rows = chunk * chunks_per_step
    blk = lambda wd: pl.BlockSpec((1, rows, wd), lambda bi, hi, ci: (bi, ci, hi))
    st_blk = pl.BlockSpec((1, 1, dk, dv), lambda bi, hi, ci: (bi, hi, 0, 0))
    in_specs = [_SMEM, blk(dk), blk(dk), blk(dv), blk(dv)]
    args = [logg, q, k, v, g]
    if s0 is not None:
        in_specs.append(st_blk)
        args.append(s0)
    kern = functools.partial(_retention_kernel, chunk=chunk, n_chunks=chunks_per_step,
                             has_init=s0 is not None)
    c = max(chunk, BF16_ROWS)
    return pl.pallas_call(
        kern,
        out_shape=[jax.ShapeDtypeStruct((b, s, nh * dv), out_dtype),
                   jax.ShapeDtypeStruct((b, nh, dk, dv), F32)],
        grid=(b, nh, s // rows),
        in_specs=in_specs,
        out_specs=[blk(dv), st_blk],
        scratch_shapes=[pltpu.VMEM((c, c), F32)],
        compiler_params=_cparams(("parallel", "parallel", "arbitrary")),
        name="retention",
    )(*args)


def _rope_tables(pos, half):
    inv_freq = 1.0 / (ROPE_BASE ** jnp.linspace(0.0, 1.0, half, dtype=F32))
    ang = pos.astype(F32)[:, None] * inv_freq[None, :]
    return jnp.cos(ang), jnp.sin(ang)


def kernel(x_prompt, x_sample, cache_k, cache_v, page_table, state_ret, norm_pre_attn, w_in_attn, lambda_q1, lambda_k1, lambda_q2, lambda_k2, subln_w_attn, w_out_attn, norm_post_attn, norm_pre_ret, w_in_ret, w_out_ret, norm_post_ret):
    b, s, d = x_prompt.shape
    db, t, _ = x_sample.shape
    nh_a = cache_k.shape[2]
    nh_r = state_ret.shape[1]
    dk_r, dv_r = state_ret.shape[2:]
    past = page_table.shape[1] * cache_k.shape[1]
    wa = nh_a * HEAD_W
    wqk_r, wv_r = nh_r * dk_r, nh_r * dv_r

    slopes = jnp.exp2(-8.0 * jnp.arange(1, nh_a + 1, dtype=F32) / nh_a)
    logg = jnp.log1p(-jnp.exp2(-5.0 - jnp.arange(nh_r, dtype=F32)))
    lams = (lambda_q1, lambda_k1, lambda_q2, lambda_k2)
    xp = x_prompt.reshape(b * s, d)
    xs = x_sample.reshape(db * t, d)

    segs_s = [(wa, [(F32, "q")]), (wa, [(F32, "k")]), (wa, [(F32, "v")]), (wa, [(F32, "g")])]
    qs, ks, vs, gs, *w_in_a = _norm_inproj(xs, norm_pre_attn, w_in_attn, segs_s, tm=1024,
                                           n_col_tiles=8, name="inproj_attn_sample")
    r3 = lambda a, n: a.reshape(n, -1, a.shape[-1])
    gated_s = _attn_sample(page_table, slopes, r3(qs, db), ks.reshape(db, t, nh_a, HEAD_W),
                           vs.reshape(db, t, nh_a, HEAD_W), r3(gs, db), cache_k, cache_v,
                           subln_w_attn, lams, pages_per_step=8)
    y1s, w_out_a = _outproj_norm_residual_cast(gated_s.reshape(db * t, wa), w_out_attn, xs,
                                               norm_post_attn, tk=512, name="outproj_attn_sample")

    segs_p = [(wa, [(BF16, "q_log2")]), (wa, [(F32, "k"), (BF16, "kb")]),
              (wa, [(F32, "v"), (BF16, "vb")]), (wa, [(F32, "g")])]
    qp, kp, kbp, vp, vbp, gp = _norm_inproj(xp, norm_pre_attn, w_in_a, segs_p, tm=1024, n_col_tiles=8,
                                            run_after=(y1s,), name="inproj_attn_prompt")
    gated_p = _attn_prompt(slopes, r3(qp, b), r3(kbp, b), r3(vbp, b), r3(gp, b), subln_w_attn, lams,
                           tq=512, tk=512, tsub=128)
    y1p = _outproj_norm_residual(gated_p.reshape(b * s, wa), w_out_a, xp, norm_post_attn,
                                 tm=512, tsub=128, name="outproj_attn_prompt")

    cos_p, sin_p = _rope_tables(jnp.arange(s, dtype=jnp.int32), dk_r // 2)
    cos_s, sin_s = _rope_tables(past + jnp.arange(t, dtype=jnp.int32), dk_r // 2)
    cos_s, sin_s = jnp.tile(cos_s, (db, 1)), jnp.tile(sin_s, (db, 1))
    rs = (1.0, dk_r ** -0.5)
    segs_rs = [(wqk_r, [(F32, "rot_q")]), (wqk_r, [(F32, "rot_k")]),
               (wv_r, [(F32, "v")]), (wv_r, [(F32, "g")])]
    sq, sk, sv, sg, *w_in_r = _norm_inproj(y1s, norm_pre_ret, w_in_ret, segs_rs, tm=1024, n_col_tiles=8,
                                           cos=cos_s, sin=sin_s, rotate_scale=rs,
                                           name="inproj_ret_sample")
    segs_rp = [(wqk_r, [(BF16, "rot_q")]), (wqk_r, [(BF16, "rot_k")]),
               (wv_r, [(BF16, "v")]), (wv_r, [(F32, "g")])]
    rq, rk, rv, rg = _norm_inproj(y1p, norm_pre_ret, w_in_r, segs_rp, tm=1024, n_col_tiles=8,
                                  cos=cos_p, sin=sin_p, rotate_scale=rs, name="inproj_ret_prompt")

    gated_rp, state_p = _retention(logg, r3(rq, b), r3(rk, b), r3(rv, b), r3(rg, b), None,
                                   chunk=256, chunks_per_step=8, out_dtype=BF16)
    gated_rs, state_s = _retention(logg, r3(sq, db), r3(sk, db), r3(sv, db), r3(sg, db),
                                   state_ret.astype(F32), chunk=t, chunks_per_step=1, out_dtype=F32)

    y2s, w_out_r = _outproj_norm_residual_cast(gated_rs.reshape(db * t, wv_r), w_out_ret, y1s,
                                               norm_post_ret, tk=512, name="outproj_ret_sample")
    y2p = _outproj_norm_residual(gated_rp.reshape(b * s, wv_r), w_out_r, y1p, norm_post_ret,
                                 tm=512, tsub=128, name="outproj_ret_prompt")

    return (y2p.reshape(b, s, d), y2s.reshape(db, t, d),
            kp.reshape(b, s, nh_a, HEAD_W), vp.reshape(b, s, nh_a, HEAD_W), state_p,
            ks.reshape(db, t, nh_a, HEAD_W), vs.reshape(db, t, nh_a, HEAD_W), state_s)
```
